```python
import jax, jax.numpy as jnp
from jax import lax
import numpy as np

D_MODEL = 4096
BATCH = 8
SEQ = 2048
DEPTH = 1
DEC_BATCH = 4
DEC_SEQ = 4096
PAST_LEN = 128

EXPAND = 2
MIX_WIDTH = EXPAND * D_MODEL
W_POOL = MIX_WIDTH // 2
W_CONV = MIX_WIDTH // 2
POOL_WINDOWS = (2, 4, 8, 16)
N_POOL_GROUPS = len(POOL_WINDOWS)
POOL_GROUP = W_POOL // N_POOL_GROUPS
CONV_WIDTH = 31
CONV_PAD = CONV_WIDTH // 2
N_IN = 2 * W_POOL + 3 * W_CONV + 2 * D_MODEL
SPLITS = tuple(np.cumsum([W_POOL, W_POOL, W_CONV, W_CONV, W_CONV, D_MODEL])[:].tolist())
RMS_EPS = 1e-6
LN_EPS = 1e-5

kernel_name = "hybrid_pool_conformer_gated_encoder"


def rmsnorm(x, g):
    xf = x.astype(jnp.float32)
    xf = xf * lax.rsqrt(jnp.mean(xf * xf, axis=-1, keepdims=True) + RMS_EPS)
    return xf.astype(x.dtype) * g


def layernorm(x, g, b):
    xf = x.astype(jnp.float32)
    mu = jnp.mean(xf, axis=-1, keepdims=True)
    xc = xf - mu
    var = jnp.mean(xc * xc, axis=-1, keepdims=True)
    return (xc * lax.rsqrt(var + LN_EPS)).astype(x.dtype) * g + b


def centred_mean(u, k):
    S = u.shape[1]
    c = jnp.cumsum(u.astype(jnp.float32), axis=1)
    c = jnp.pad(c, ((0, 0), (1, 0), (0, 0)))
    t = jnp.arange(S)
    lo = jnp.clip(t - k // 2, 0, S - 1)
    hi = jnp.clip(t + k - k // 2 - 1, 0, S - 1)
    s = jnp.take(c, hi + 1, axis=1) - jnp.take(c, lo, axis=1)
    cnt = (hi - lo + 1).astype(jnp.float32)[None, :, None]
    return (s / cnt).astype(u.dtype)


def pool_branch(u, w_pool, ls_pool):
    B, S, _ = u.shape
    ug = u.reshape(B, S, N_POOL_GROUPS, POOL_GROUP)
    pooled = jnp.stack([centred_mean(ug[:, :, gi, :], k) for gi, k in enumerate(POOL_WINDOWS)], axis=2)
    mixed = pooled - ug
    mixed = jnp.einsum('bsgc,gcd->bsgd', mixed, w_pool)
    return mixed.reshape(B, S, W_POOL) * ls_pool


def conv_branch(a, b, conv_w, conv_b, ln_g, ln_b):
    v = a * jax.nn.sigmoid(b)
    v = lax.conv_general_dilated(
        v, conv_w[:, None, :].astype(v.dtype), window_strides=(1,),
        padding=[(CONV_PAD, CONV_PAD)], dimension_numbers=('NWC', 'WIO', 'NWC'),
        feature_group_count=W_CONV) + conv_b
    v = layernorm(v, ln_g, ln_b)
    return jax.nn.silu(v)


def mixer_layer(x, g_norm, w_in, w_pool, ls_pool, w_proj_pool, conv_w, conv_b, ln_g, ln_b,
                w_proj_conv, b_proj_conv, w_out):
    h = rmsnorm(x, g_norm)
    p = jnp.einsum('bsd,dn->bsn', h, w_in)
    u_pool, z_pool, a_conv, b_conv, z_conv, g_pool, g_conv = jnp.split(p, SPLITS, axis=-1)
    y_pool = pool_branch(u_pool, w_pool, ls_pool) * jax.nn.silu(z_pool)
    y_pool = jnp.einsum('bsw,wd->bsd', y_pool, w_proj_pool)
    y_conv = conv_branch(a_conv, b_conv, conv_w, conv_b, ln_g, ln_b) * jax.nn.silu(z_conv)
    y_conv = jnp.einsum('bsw,wd->bsd', y_conv, w_proj_conv) + b_proj_conv
    merged = jax.nn.sigmoid(g_pool) * y_pool + jax.nn.sigmoid(g_conv) * y_conv
    return x + jnp.einsum('bsd,de->bse', merged, w_out)


def trunk(x, g_norm, w_in, w_pool, ls_pool, w_proj_pool, conv_w, conv_b, ln_g, ln_b,
          w_proj_conv, b_proj_conv, w_out, g_final):
    for l in range(DEPTH):
        x = mixer_layer(x, g_norm[l], w_in[l], w_pool[l], ls_pool[l], w_proj_pool[l], conv_w[l],
                        conv_b[l], ln_g[l], ln_b[l], w_proj_conv[l], b_proj_conv[l], w_out[l])
    return rmsnorm(x, g_final)


def setup_inputs(seed: int = 0) -> dict:
    key = jax.random.key(seed)
    ks = jax.random.split(key, 16)
    f32 = jnp.float32
    nrm = lambda k, shape, s: jax.random.normal(k, shape, f32) * s
    return {
        "x_prompt": nrm(ks[0], (BATCH, SEQ, D_MODEL), 1.0),
        "x_sample": nrm(ks[1], (DEC_BATCH, DEC_SEQ, D_MODEL), 1.0),
        "g_norm": 1.0 + nrm(ks[2], (DEPTH, D_MODEL), 0.02),
        "w_in": nrm(ks[3], (DEPTH, D_MODEL, N_IN), D_MODEL ** -0.5),
        "w_pool": nrm(ks[4], (DEPTH, N_POOL_GROUPS, POOL_GROUP, POOL_GROUP), POOL_GROUP ** -0.5),
        "ls_pool": 1.0 + nrm(ks[5], (DEPTH, W_POOL), 0.02),
        "w_proj_pool": nrm(ks[6], (DEPTH, W_POOL, D_MODEL), W_POOL ** -0.5),
        "conv_w": nrm(ks[7], (DEPTH, CONV_WIDTH, W_CONV), CONV_WIDTH ** -0.5),
        "conv_b": nrm(ks[8], (DEPTH, W_CONV), 0.02),
        "ln_g": 1.0 + nrm(ks[9], (DEPTH, W_CONV), 0.02),
        "ln_b": nrm(ks[10], (DEPTH, W_CONV), 0.02),
        "w_proj_conv": nrm(ks[11], (DEPTH, W_CONV, D_MODEL), W_CONV ** -0.5),
        "b_proj_conv": nrm(ks[12], (DEPTH, D_MODEL), 0.02),
        "w_out": nrm(ks[13], (DEPTH, D_MODEL, D_MODEL), D_MODEL ** -0.5),
        "g_final": 1.0 + nrm(ks[14], (D_MODEL,), 0.02),
    }


def reference(x_prompt, x_sample, g_norm, w_in, w_pool, ls_pool, w_proj_pool, conv_w, conv_b,
              ln_g, ln_b, w_proj_conv, b_proj_conv, w_out, g_final):
    y_prompt = trunk(x_prompt, g_norm, w_in, w_pool, ls_pool, w_proj_pool, conv_w, conv_b,
                     ln_g, ln_b, w_proj_conv, b_proj_conv, w_out, g_final)
    y_sample = trunk(x_sample, g_norm, w_in, w_pool, ls_pool, w_proj_pool, conv_w, conv_b,
                     ln_g, ln_b, w_proj_conv, b_proj_conv, w_out, g_final)
    return (y_prompt, y_sample)
```

```python
import functools

import jax
import jax.numpy as jnp
from jax import lax
from jax.experimental import pallas as pl
from jax.experimental.pallas import tpu as pltpu

F32 = jnp.float32
BF16 = jnp.bfloat16

RMS_EPS = 1e-6
LN_EPS = 1e-5
POOL_WINDOWS = (2, 4, 8, 16)

LANES = 128
HALO = 16
ROW_CHUNK = 64
ROW_STRIDE = 4
VMEM_LIMIT_BYTES = 56 * 1024 * 1024


def _params(semantics):
    return pltpu.CompilerParams(dimension_semantics=semantics, vmem_limit_bytes=VMEM_LIMIT_BYTES)


def _tile(dim, target):
    t = min(dim, target)
    while dim % t:
        t -= 1
    return t


def _sigmoid(x):
    return 1.0 / (1.0 + jnp.exp(-x))


def _silu(x):
    return x * _sigmoid(x)


def _row_loop(n_rows, chunk, body):
    def step(c, carry):
        body(pl.multiple_of(c * chunk, chunk))
        return carry
    lax.fori_loop(0, n_rows // chunk, step, 0)


def _static_row_loop(n_rows, chunk, body):
    for r in range(0, n_rows, chunk):
        body(r)


def _inproj_kernel(x_ref, g_ref, w_ref, o_ref, h_ref):
    tm = x_ref.shape[0]

    @pl.when(pl.program_id(1) == 0)
    def _():
        def body(r):
            x = x_ref[pl.ds(r, 16), :]
            ms = jnp.mean(x * x, axis=-1, keepdims=True)
            h_ref[pl.ds(r, 16), :] = ((x * lax.rsqrt(ms + RMS_EPS)) * g_ref[...]).astype(BF16)
        _row_loop(tm, 16, body)

    o_ref[...] = jnp.dot(h_ref[...], w_ref[...], preferred_element_type=F32).astype(o_ref.dtype)


def _inproj(x, g_norm, w_in):
    t, d = x.shape
    n = w_in.shape[1]
    tm, tn = _tile(t, 512), _tile(n, 1024)
    return pl.pallas_call(
        _inproj_kernel,
        out_shape=jax.ShapeDtypeStruct((t, n), BF16),
        grid=(t // tm, n // tn),
        in_specs=[
            pl.BlockSpec((tm, d), lambda i, j: (i, 0)),
            pl.BlockSpec((1, d), lambda i, j: (0, 0)),
            pl.BlockSpec((d, tn), lambda i, j: (0, j)),
        ],
        out_specs=pl.BlockSpec((tm, tn), lambda i, j: (i, j)),
        scratch_shapes=[pltpu.VMEM((tm, d), BF16)],
        compiler_params=_params(("parallel", "arbitrary")),
        name="inproj",
    )(x, g_norm, w_in)


def _seqmix_kernel(a_ref, ap_ref, an_ref, b_ref, bp_ref, bn_ref, u_ref, up_ref, un_ref,
                   cw_ref, cb_ref, cv_ref, mx_ref, v_scr, u_scr, w_scr, cvf_scr, mxf_scr,
                   *, seq_len, group_size):
    tr, tc = a_ref.shape
    n_lb = tc // LANES
    conv_width = cw_ref.shape[0]
    conv_pad = conv_width // 2
    i = pl.program_id(0)
    j = pl.program_id(1)
    pos0 = (i * tr) % seq_len
    keep_prev = jnp.where(pos0 != 0, 1.0, 0.0).astype(F32)
    keep_next = jnp.where(pos0 + tr != seq_len, 1.0, 0.0).astype(F32)

    def glu(a, b):
        return a.astype(F32) * _sigmoid(b.astype(F32))

    for lb in range(n_lb):
        lanes = slice(lb * LANES, (lb + 1) * LANES)
        v_scr[lb, 0:HALO, :] = glu(ap_ref[:, lanes], bp_ref[:, lanes]) * keep_prev
        v_scr[lb, HALO + tr:, :] = glu(an_ref[:, lanes], bn_ref[:, lanes]) * keep_next
        u_scr[lb, 0:HALO, :] = up_ref[:, lanes].astype(F32) * keep_prev
        u_scr[lb, HALO + tr:, :] = un_ref[:, lanes].astype(F32) * keep_next
    for tap in range(conv_width):
        w_scr[tap] = jnp.broadcast_to(cw_ref[tap:tap + 1, :], (8, tc))
    w_scr[conv_width] = jnp.broadcast_to(cb_ref[...], (8, tc))

    def fill(r):
        rows = pl.ds(r, ROW_CHUNK)
        for lb in range(n_lb):
            lanes = slice(lb * LANES, (lb + 1) * LANES)
            v_scr[lb, pl.ds(HALO + r, ROW_CHUNK), :] = glu(a_ref[rows, lanes], b_ref[rows, lanes])
            u_scr[lb, pl.ds(HALO + r, ROW_CHUNK), :] = u_ref[rows, lanes].astype(F32)
    _row_loop(tr, ROW_CHUNK, fill)

    n_vreg = ROW_CHUNK // (8 * ROW_STRIDE)

    def strided(ref, lb, start):
        return ref[lb, pl.ds(start, 8 * n_vreg, stride=ROW_STRIDE), :]

    def conv(r):
        for lb in range(n_lb):
            lanes = slice(lb * LANES, (lb + 1) * LANES)
            bias = w_scr[conv_width, :, lanes]
            acc = [[bias] * n_vreg for _ in range(ROW_STRIDE)]
            for s in range(conv_width + ROW_STRIDE - 1):
                x = strided(v_scr, lb, HALO - conv_pad + r + s)
                for c in range(ROW_STRIDE):
                    tap = s - c
                    if 0 <= tap < conv_width:
                        w = w_scr[tap, :, lanes]
                        for q in range(n_vreg):
                            acc[c][q] = acc[c][q] + w * x[8 * q:8 * q + 8]
            for c in range(ROW_STRIDE):
                cvf_scr[lb, pl.ds(r + c, 8 * n_vreg, stride=ROW_STRIDE), :] = jnp.concatenate(acc[c], axis=0)
    _static_row_loop(tr, ROW_CHUNK, conv)

    group = (j * tc) // group_size
    for gi, k in enumerate(POOL_WINDOWS):
        @pl.when(group == gi)
        def _(k=k):
            def pool(r):
                inv_cnt = []
                for c in range(ROW_STRIDE):
                    row = lax.broadcasted_iota(jnp.int32, (8 * n_vreg, LANES), 0)
                    pos = pos0 + (r + c) + ROW_STRIDE * row
                    lo = jnp.maximum(pos - k // 2, 0)
                    hi = jnp.minimum(pos + (k - k // 2 - 1), seq_len - 1)
                    inv_cnt.append(1.0 / (hi - lo + 1).astype(F32))
                for lb in range(n_lb):
                    tot = [None] * ROW_STRIDE
                    mid = [None] * ROW_STRIDE
                    for s in range(k + ROW_STRIDE - 1):
                        x = strided(u_scr, lb, HALO - k // 2 + r + s)
                        for c in range(ROW_STRIDE):
                            if 0 <= s - c < k:
                                tot[c] = x if tot[c] is None else tot[c] + x
                            if s - c == k // 2:
                                mid[c] = x
                    for c in range(ROW_STRIDE):
                        mxf_scr[lb, pl.ds(r + c, 8 * n_vreg, stride=ROW_STRIDE), :] = tot[c] * inv_cnt[c] - mid[c]
            _static_row_loop(tr, ROW_CHUNK, pool)

    def emit(r):
        rows = pl.ds(r, ROW_CHUNK)
        for lb in range(n_lb):
            lanes = slice(lb * LANES, (lb + 1) * LANES)
            cv_ref[rows, lanes] = cvf_scr[lb, rows, :].astype(cv_ref.dtype)
            mx_ref[rows, lanes] = mxf_scr[lb, rows, :].astype(mx_ref.dtype)
    _row_loop(tr, ROW_CHUNK, emit)


def _seqmix(p, conv_w, conv_b, *, seq_len, width, group_size):
    t = p.shape[0]
    tr, tc = _tile(seq_len, 512), _tile(group_size, 512)
    nc = width // tc
    hb = tr // HALO
    last_hb = t // HALO - 1

    def main(seg):
        return pl.BlockSpec((tr, tc), lambda i, j: (i, seg * nc + j))

    def prev(seg):
        return pl.BlockSpec((HALO, tc), lambda i, j: (jnp.maximum(i * hb - 1, 0), seg * nc + j))

    def nxt(seg):
        return pl.BlockSpec((HALO, tc), lambda i, j: (jnp.minimum((i + 1) * hb, last_hb), seg * nc + j))

    seg_u, seg_a, seg_b = 0, 2, 3
    kernel = functools.partial(_seqmix_kernel, seq_len=seq_len, group_size=group_size)
    return pl.pallas_call(
        kernel,
        out_shape=(jax.ShapeDtypeStruct((t, width), BF16), jax.ShapeDtypeStruct((t, width), BF16)),
        grid=(t // tr, nc),
        in_specs=[main(seg_a), prev(seg_a), nxt(seg_a), main(seg_b), prev(seg_b), nxt(seg_b),
                  main(seg_u), prev(seg_u), nxt(seg_u),
                  pl.BlockSpec((conv_w.shape[0], tc), lambda i, j: (0, j)),
                  pl.BlockSpec((1, tc), lambda i, j: (0, j))],
        out_specs=(pl.BlockSpec((tr, tc), lambda i, j: (i, j)),
                   pl.BlockSpec((tr, tc), lambda i, j: (i, j))),
        scratch_shapes=[pltpu.VMEM((tc // LANES, tr + 2 * HALO, LANES), F32),
                        pltpu.VMEM((tc // LANES, tr + 2 * HALO, LANES), F32),
                        pltpu.VMEM((conv_w.shape[0] + 1, 8, tc), F32),
                        pltpu.VMEM((tc // LANES, tr, LANES), F32),
                        pltpu.VMEM((tc // LANES, tr, LANES), F32)],
        compiler_params=_params(("parallel", "parallel")),
        name="seqmix",
    )(p, p, p, p, p, p, p, p, p, conv_w, conv_b)


def _poolmix_kernel(m_ref, w_ref, z_ref, ls_ref, o_ref):
    y = jnp.dot(m_ref[...], w_ref[0], preferred_element_type=F32)
    o_ref[...] = ((y * ls_ref[...]) * _silu(z_ref[...].astype(F32))).astype(o_ref.dtype)


def _poolmix(mixed, w_pool, p, ls_pool):
    t, width = mixed.shape
    n_groups, gs, _ = w_pool.shape
    tm = _tile(t, 1024)
    seg_z = 1
    return pl.pallas_call(
        _poolmix_kernel,
        out_shape=jax.ShapeDtypeStruct((t, width), BF16),
        grid=(t // tm, n_groups),
        in_specs=[
            pl.BlockSpec((tm, gs), lambda i, g: (i, g)),
            pl.BlockSpec((1, gs, gs), lambda i, g: (g, 0, 0)),
            pl.BlockSpec((tm, gs), lambda i, g: (i, seg_z * n_groups + g)),
            pl.BlockSpec((1, gs), lambda i, g: (0, g)),
        ],
        out_specs=pl.BlockSpec((tm, gs), lambda i, g: (i, g)),
        compiler_params=_params(("parallel", "parallel")),
        name="poolmix",
    )(mixed, w_pool, p, ls_pool)


def _convnorm_kernel(cv_ref, z_ref, g_ref, b_ref, o_ref):
    def body(r):
        v = cv_ref[pl.ds(r, 16), :].astype(F32)
        mu = jnp.mean(v, axis=-1, keepdims=True)
        xc = v - mu
        var = jnp.mean(xc * xc, axis=-1, keepdims=True)
        y = (xc * lax.rsqrt(var + LN_EPS)) * g_ref[...] + b_ref[...]
        z = z_ref[pl.ds(r, 16), :].astype(F32)
        o_ref[pl.ds(r, 16), :] = (_silu(y) * _silu(z)).astype(o_ref.dtype)
    _row_loop(cv_ref.shape[0], 16, body)


def _convnorm(cv, p, ln_g, ln_b):
    t, width = cv.shape
    tr = _tile(t, 256)
    seg_z = 4
    return pl.pallas_call(
        _convnorm_kernel,
        out_shape=jax.ShapeDtypeStruct((t, width), BF16),
        grid=(t // tr,),
        in_specs=[
            pl.BlockSpec((tr, width), lambda i: (i, 0)),
            pl.BlockSpec((tr, width), lambda i: (i, seg_z)),
            pl.BlockSpec((1, width), lambda i: (0, 0)),
            pl.BlockSpec((1, width), lambda i: (0, 0)),
        ],
        out_specs=pl.BlockSpec((tr, width), lambda i: (i, 0)),
        compiler_params=_params(("parallel",)),
        name="convnorm",
    )(cv, p, ln_g, ln_b)


def _proj_kernel(ap_ref, ac_ref, wpp_ref, wpc_ref, bc_ref, gp_ref, gc_ref, o_ref):
    y_pool = jnp.dot(ap_ref[...], wpp_ref[...], preferred_element_type=F32)
    y_conv = jnp.dot(ac_ref[...], wpc_ref[...], preferred_element_type=F32) + bc_ref[...]
    merged = _sigmoid(gp_ref[...].astype(F32)) * y_pool + _sigmoid(gc_ref[...].astype(F32)) * y_conv
    o_ref[...] = merged.astype(o_ref.dtype)


def _proj(act_pool, act_conv, wpp, wpc, b_proj_conv, p):
    t, width = act_pool.shape
    d = wpp.shape[1]
    tm, tn = _tile(t, 512), _tile(d, 512)
    nd = d // tn
    seg_gp = (5 * width) // tn
    seg_gc = seg_gp + nd
    return pl.pallas_call(
        _proj_kernel,
        out_shape=jax.ShapeDtypeStruct((t, d), BF16),
        grid=(t // tm, nd),
        in_specs=[
            pl.BlockSpec((tm, width), lambda i, j: (i, 0)),
            pl.BlockSpec((tm, width), lambda i, j: (i, 0)),
            pl.BlockSpec((width, tn), lambda i, j: (0, j)),
            pl.BlockSpec((width, tn), lambda i, j: (0, j)),
            pl.BlockSpec((1, tn), lambda i, j: (0, j)),
            pl.BlockSpec((tm, tn), lambda i, j: (i, seg_gp + j)),
            pl.BlockSpec((tm, tn), lambda i, j: (i, seg_gc + j)),
        ],
        out_specs=pl.BlockSpec((tm, tn), lambda i, j: (i, j)),
        compiler_params=_params(("parallel", "arbitrary")),
        name="proj",
    )(act_pool, act_conv, wpp, wpc, b_proj_conv, p, p)


def _out_kernel(m_ref, w_ref, x_ref, g_ref, o_ref, y_scr):
    nd, tm, tn = y_scr.shape
    j = pl.program_id(1)
    y_scr[j] = x_ref[...] + jnp.dot(m_ref[...], w_ref[...], preferred_element_type=F32)

    @pl.when(j == nd - 1)
    def _():
        def body(r):
            rows = pl.ds(r, 16)
            ss = jnp.zeros((16, 1), F32)
            for c in range(nd):
                v = y_scr[c, rows, :]
                ss = ss + jnp.sum(v * v, axis=-1, keepdims=True)
            inv = lax.rsqrt(ss / (nd * tn) + RMS_EPS)
            for c in range(nd):
                cols = slice(c * tn, (c + 1) * tn)
                o_ref[rows, cols] = (y_scr[c, rows, :] * inv) * g_ref[:, cols]
        _row_loop(tm, 16, body)


def _out(merged, w_out, x, g_final):
    t, d = x.shape
    tm, tn = _tile(t, 512), _tile(d, 512)
    return pl.pallas_call(
        _out_kernel,
        out_shape=jax.ShapeDtypeStruct((t, d), F32),
        grid=(t // tm, d // tn),
        in_specs=[
            pl.BlockSpec((tm, d), lambda i, j: (i, 0)),
            pl.BlockSpec((d, tn), lambda i, j: (0, j)),
            pl.BlockSpec((tm, tn), lambda i, j: (i, j)),
            pl.BlockSpec((1, d), lambda i, j: (0, 0)),
        ],
        out_specs=pl.BlockSpec((tm, d), lambda i, j: (i, 0)),
        scratch_shapes=[pltpu.VMEM((d // tn, tm, tn), F32)],
        compiler_params=_params(("parallel", "arbitrary")),
        name="outproj",
    )(merged, w_out, x, g_final)


def _layer(x, weights):
    (g_norm, w_in, w_pool, ls_pool, wpp, conv_w, conv_b, ln_g, ln_b, wpc, b_proj_conv, w_out,
     g_final) = weights
    batch, seq_len, d = x.shape
    width = wpp.shape[0]
    group_size = w_pool.shape[1]
    xt = x.reshape(batch * seq_len, d)
    p = _inproj(xt, g_norm, w_in)
    cv, mixed = _seqmix(p, conv_w, conv_b, seq_len=seq_len, width=width, group_size=group_size)
    act_pool = _poolmix(mixed, w_pool, p, ls_pool)
    act_conv = _convnorm(cv, p, ln_g, ln_b)
    merged = _proj(act_pool, act_conv, wpp, wpc, b_proj_conv, p)
    return _out(merged, w_out, xt, g_final).reshape(batch, seq_len, d)


def kernel(x_prompt, x_sample, g_norm, w_in, w_pool, ls_pool, w_proj_pool, conv_w, conv_b, ln_g, ln_b,
           w_proj_conv, b_proj_conv, w_out, g_final):
    depth = w_in.shape[0]
    d = x_prompt.shape[-1]
    width = w_proj_pool.shape[1]
    assert depth == 1 and w_proj_conv.shape[1] == width == d, "segment layout assumes equal branch widths"
    assert w_in.shape[2] == 5 * width + 2 * d and conv_w.shape[1] // 2 < HALO
    assert w_pool.shape[1] == len(POOL_WINDOWS) and max(POOL_WINDOWS) // 2 <= HALO
    weights = (
        g_norm[0][None, :], w_in[0].astype(BF16), w_pool[0].astype(BF16), ls_pool[0][None, :],
        w_proj_pool[0].astype(BF16), conv_w[0], conv_b[0][None, :], ln_g[0][None, :], ln_b[0][None, :],
        w_proj_conv[0].astype(BF16), b_proj_conv[0][None, :], w_out[0].astype(BF16), g_final[None, :],
    )
    return (_layer(x_prompt, weights), _layer(x_sample, weights))
```

```python
import functools

import jax
import jax.numpy as jnp
from jax import lax
from jax.experimental import pallas as pl
from jax.experimental.pallas import tpu as pltpu

F32 = jnp.float32
BF16 = jnp.bfloat16

RMS_EPS = 1e-6
LN_EPS = 1e-5
POOL_WINDOWS = (2, 4, 8, 16)

LANES = 128
SUBLANES = 8
HALO = 16
ROW_CHUNK = 64
ROW_STRIDE = 4
CONV_CLASSES = 1
VMEM_LIMIT_BYTES = 56 * 1024 * 1024


def _params(semantics):
    return pltpu.CompilerParams(dimension_semantics=semantics, vmem_limit_bytes=VMEM_LIMIT_BYTES)


def _tile(dim, target):
    t = min(dim, target)
    while dim % t:
        t -= 1
    return t


def _sigmoid(x):
    return 0.5 * jnp.tanh(0.5 * x) + 0.5


def _silu(x):
    return x * _sigmoid(x)


def _zero_after(x):
    bits = pltpu.bitcast(x, jnp.uint32)
    return lax.shift_right_logical(lax.shift_right_logical(bits, jnp.uint32(16)), jnp.uint32(16)).astype(F32)


def _row_loop(n_rows, chunk, body):
    def step(c, carry):
        body(pl.multiple_of(c * chunk, chunk))
        return carry
    lax.fori_loop(0, n_rows // chunk, step, 0)


def _inproj_seq_kernel(x_ref, g_ref, w_ref, o_ref, h_ref):
    tm = x_ref.shape[0]

    @pl.when(pl.program_id(1) == 0)
    def _():
        def body(r):
            x = x_ref[pl.ds(r, 16), :]
            ms = jnp.mean(x * x, axis=-1, keepdims=True)
            h_ref[pl.ds(r, 16), :] = ((x * lax.rsqrt(ms + RMS_EPS)) * g_ref[...]).astype(BF16)
        _row_loop(tm, 16, body)

    o_ref[...] = jnp.dot(h_ref[...], w_ref[...], preferred_element_type=F32).astype(o_ref.dtype)


def _inproj_seq(x, g_norm, w_in, width):
    t, d = x.shape
    tm, tn = _tile(t, 512), _tile(width, 1024)
    nseg = width // tn
    w_col = lambda i, j: (0, j + jnp.where(j >= nseg, nseg, 0))
    return pl.pallas_call(
        _inproj_seq_kernel,
        out_shape=(jax.ShapeDtypeStruct((t, 3 * width), BF16), jax.ShapeDtypeStruct((t, d), BF16)),
        grid=(t // tm, 3 * nseg),
        in_specs=[
            pl.BlockSpec((tm, d), lambda i, j: (i, 0)),
            pl.BlockSpec((1, d), lambda i, j: (0, 0)),
            pl.BlockSpec((d, tn), w_col),
        ],
        out_specs=(pl.BlockSpec((tm, tn), lambda i, j: (i, j)),
                   pl.BlockSpec((tm, d), lambda i, j: (i, 0))),
        compiler_params=_params(("parallel", "arbitrary")),
        name="inproj_seq",
    )(x, g_norm, w_in)


def _pool_rings():
    rings, seen = [], set()
    for k in POOL_WINDOWS:
        offs = [d for d in range(-(k // 2), k - k // 2) if d not in seen]
        seen.update(offs)
        rings.append(offs)
    assert len(seen) == max(POOL_WINDOWS), "pooling windows must be nested"
    return rings


def _seqmix(a_ref, ap_ref, an_ref, b_ref, bp_ref, bn_ref, u_ref, up_ref, un_ref, cw_ref, cb_ref,
            cv_ref, mx_ref, v_scr, u_scr, w_scr, cvf_scr, mxf_scr, *, seq_len, group_size):
    tr, tc = a_ref.shape
    n_lb = tc // LANES
    conv_width = cw_ref.shape[0]
    conv_pad = conv_width // 2
    n_vreg = ROW_CHUNK // (SUBLANES * ROW_STRIDE)
    pos0 = (pl.program_id(0) * tr) % seq_len
    keep_prev = jnp.where(pos0 != 0, 1.0, 0.0).astype(F32)
    keep_next = jnp.where(pos0 + tr != seq_len, 1.0, 0.0).astype(F32)
    group = (pl.program_id(1) * tc) // group_size

    def glu(a, b):
        return a.astype(F32) * _sigmoid(b.astype(F32))

    for lb in range(n_lb):
        lanes = slice(lb * LANES, (lb + 1) * LANES)
        v_scr[lb, 0:HALO, :] = glu(ap_ref[:, lanes], bp_ref[:, lanes]) * keep_prev
        v_scr[lb, HALO + tr:, :] = glu(an_ref[:, lanes], bn_ref[:, lanes]) * keep_next
        u_scr[lb, 0:HALO, :] = up_ref[:, lanes].astype(F32) * keep_prev
        u_scr[lb, HALO + tr:, :] = un_ref[:, lanes].astype(F32) * keep_next
        for r in range(0, tr, ROW_CHUNK):
            rows = slice(r, r + ROW_CHUNK)
            dst = slice(HALO + r, HALO + r + ROW_CHUNK)
            v_scr[lb, dst, :] = glu(a_ref[rows, lanes], b_ref[rows, lanes])
            u_scr[lb, dst, :] = u_ref[rows, lanes].astype(F32)
    for tap in range(conv_width):
        w_scr[tap] = jnp.broadcast_to(cw_ref[tap:tap + 1, :], (SUBLANES, tc))
    w_scr[conv_width] = jnp.broadcast_to(cb_ref[...], (SUBLANES, tc))

    def strided(ref, lb, start):
        return ref[lb, pl.ds(start, SUBLANES * n_vreg, stride=ROW_STRIDE), :]

    def put(ref, lb, start, val):
        ref[lb, pl.ds(start, SUBLANES * n_vreg, stride=ROW_STRIDE), :] = val

    carry = jnp.zeros((SUBLANES, LANES), F32)
    for r in range(0, tr, ROW_CHUNK):
        for lb in range(n_lb):
            lanes = slice(lb * LANES, (lb + 1) * LANES)
            for c0 in range(0, ROW_STRIDE, CONV_CLASSES):
                classes = range(c0, c0 + CONV_CLASSES)
                acc = {c: [w_scr[conv_width, :, lanes] + carry] * n_vreg for c in classes}
                for s in range(conv_width + CONV_CLASSES - 1):
                    x = strided(v_scr, lb, HALO - conv_pad + r + c0 + s)
                    for c in classes:
                        tap = s - (c - c0)
                        if 0 <= tap < conv_width:
                            w = w_scr[tap, :, lanes]
                            acc[c] = [acc[c][q] + w * x[SUBLANES * q:SUBLANES * (q + 1)]
                                      for q in range(n_vreg)]
                for c in classes:
                    put(cvf_scr, lb, r + c, jnp.concatenate(acc[c], axis=0))
                carry = _zero_after(acc[c0][-1])

    rings = _pool_rings()
    k_half = jnp.int32(POOL_WINDOWS[-1] // 2)
    k_rest = jnp.int32(POOL_WINDOWS[-1] - POOL_WINDOWS[-1] // 2 - 1)
    for gi in range(len(POOL_WINDOWS) - 2, -1, -1):
        k = POOL_WINDOWS[gi]
        k_half = jnp.where(group == gi, k // 2, k_half)
        k_rest = jnp.where(group == gi, k - k // 2 - 1, k_rest)
    for r in range(0, tr, ROW_CHUNK):
        for c in range(ROW_STRIDE):
            row = lax.broadcasted_iota(jnp.int32, (SUBLANES * n_vreg, LANES), 0)
            pos = pos0 + (r + c) + ROW_STRIDE * row
            lo = jnp.maximum(pos - k_half, 0)
            hi = jnp.minimum(pos + k_rest, seq_len - 1)
            inv_cnt = 1.0 / (hi - lo + 1).astype(F32)
            for lb in range(n_lb):
                ring_sum, mid = [], None
                for offs in rings:
                    tot = None
                    for d in offs:
                        x = strided(u_scr, lb, HALO + r + c + d)
                        tot = x if tot is None else tot + x
                        if d == 0:
                            mid = x
                    ring_sum.append(tot)
                win = ring_sum[0]
                sel = win
                for gi in range(1, len(rings)):
                    win = win + ring_sum[gi]
                    sel = jnp.where(group >= gi, win, sel)
                mixed = (sel * inv_cnt - mid) + jnp.concatenate([carry] * n_vreg, axis=0)
                put(mxf_scr, lb, r + c, mixed)
                carry = _zero_after(mixed[:SUBLANES])

    for lb in range(n_lb):
        lanes = slice(lb * LANES, (lb + 1) * LANES)
        for r in range(0, tr, ROW_CHUNK):
            rows = slice(r, r + ROW_CHUNK)
            cv_ref[rows, lanes] = cvf_scr[lb, rows, :].astype(cv_ref.dtype)
            mx_ref[rows, lanes] = mxf_scr[lb, rows, :].astype(mx_ref.dtype)


def _inproj_gate_kernel(h_ref, w_ref, *refs, seq_len, group_size):
    seq_in, (p_ref, cv_ref, mx_ref), scratch = refs[:11], refs[11:14], refs[14:]
    p_ref[...] = jnp.dot(h_ref[...], w_ref[...], preferred_element_type=F32).astype(p_ref.dtype)
    _seqmix(*seq_in, cv_ref, mx_ref, *scratch, seq_len=seq_len, group_size=group_size)


def _inproj_gate(h, w_in, p_seq, conv_w, conv_b, *, seq_len, width, group_size):
    t, d = h.shape
    tm = _tile(seq_len, 512)
    tn = _tile(width, 1024)
    nseg = width // tn
    n_steps = 4 * nseg
    tc = width // n_steps
    assert tc % LANES == 0 and group_size % tc == 0 and tm % ROW_CHUNK == 0
    nc = width // tc
    hb = tm // HALO
    last_hb = t // HALO - 1
    w_col = lambda i, j: (0, nseg + j + jnp.where(j >= nseg, 2 * nseg, 0))

    def main(seg):
        return pl.BlockSpec((tm, tc), lambda i, j: (i, seg * nc + j))

    def prev(seg):
        return pl.BlockSpec((HALO, tc), lambda i, j: (jnp.maximum(i * hb - 1, 0), seg * nc + j))

    def nxt(seg):
        return pl.BlockSpec((HALO, tc), lambda i, j: (jnp.minimum((i + 1) * hb, last_hb), seg * nc + j))

    seg_u, seg_a, seg_b = 0, 1, 2
    n_lb = tc // LANES
    kernel = functools.partial(_inproj_gate_kernel, seq_len=seq_len, group_size=group_size)
    return pl.pallas_call(
        kernel,
        out_shape=(jax.ShapeDtypeStruct((t, 4 * width), BF16),
                   jax.ShapeDtypeStruct((t, width), BF16),
                   jax.ShapeDtypeStruct((t, width), BF16)),
        grid=(t // tm, n_steps),
        in_specs=[pl.BlockSpec((tm, d), lambda i, j: (i, 0)),
                  pl.BlockSpec((d, tn), w_col),
                  main(seg_a), prev(seg_a), nxt(seg_a), main(seg_b), prev(seg_b), nxt(seg_b),
                  main(seg_u), prev(seg_u), nxt(seg_u),
                  pl.BlockSpec((conv_w.shape[0], tc), lambda i, j: (0, j)),
                  pl.BlockSpec((1, tc), lambda i, j: (0, j))],
        out_specs=(pl.BlockSpec((tm, tn), lambda i, j: (i, j)),
                   pl.BlockSpec((tm, tc), lambda i, j: (i, j)),
                   pl.BlockSpec((tm, tc), lambda i, j: (i, j))),
        scratch_shapes=[pltpu.VMEM((n_lb, tm + 2 * HALO, LANES), F32),
                        pltpu.VMEM((n_lb, tm + 2 * HALO, LANES), F32),
                        pltpu.VMEM((conv_w.shape[0] + 1, SUBLANES, tc), F32),
                        pltpu.VMEM((n_lb, tm, LANES), F32),
                        pltpu.VMEM((n_lb, tm, LANES), F32)],
        compiler_params=_params(("parallel", "parallel")),
        name="inproj_gate",
    )(h, w_in, p_seq, p_seq, p_seq, p_seq, p_seq, p_seq, p_seq, p_seq, p_seq, conv_w, conv_b)


def _poolmix_kernel(m_ref, w_ref, z_ref, ls_ref, o_ref):
    y = jnp.dot(m_ref[...], w_ref[0], preferred_element_type=F32)
    o_ref[...] = ((y * ls_ref[...]) * _silu(z_ref[...].astype(F32))).astype(o_ref.dtype)


def _poolmix(mixed, w_pool, p_gate, ls_pool):
    t, width = mixed.shape
    n_groups, gs, _ = w_pool.shape
    tm = _tile(t, 1024)
    return pl.pallas_call(
        _poolmix_kernel,
        out_shape=jax.ShapeDtypeStruct((t, width), BF16),
        grid=(t // tm, n_groups),
        in_specs=[
            pl.BlockSpec((tm, gs), lambda i, g: (i, g)),
            pl.BlockSpec((1, gs, gs), lambda i, g: (g, 0, 0)),
            pl.BlockSpec((tm, gs), lambda i, g: (i, g)),
            pl.BlockSpec((1, gs), lambda i, g: (0, g)),
        ],
        out_specs=pl.BlockSpec((tm, gs), lambda i, g: (i, g)),
        compiler_params=_params(("parallel", "parallel")),
        name="poolmix",
    )(mixed, w_pool, p_gate, ls_pool)


def _convnorm_kernel(cv_ref, z_ref, g_ref, b_ref, o_ref):
    def body(r):
        v = cv_ref[pl.ds(r, 16), :].astype(F32)
        mu = jnp.mean(v, axis=-1, keepdims=True)
        xc = v - mu
        var = jnp.mean(xc * xc, axis=-1, keepdims=True)
        y = (xc * lax.rsqrt(var + LN_EPS)) * g_ref[...] + b_ref[...]
        z = z_ref[pl.ds(r, 16), :].astype(F32)
        o_ref[pl.ds(r, 16), :] = (_silu(y) * _silu(z)).astype(o_ref.dtype)
    _row_loop(cv_ref.shape[0], 16, body)


def _convnorm(cv, p_gate, ln_g, ln_b):
    t, width = cv.shape
    tr = _tile(t, 256)
    return pl.pallas_call(
        _convnorm_kernel,
        out_shape=jax.ShapeDtypeStruct((t, width), BF16),
        grid=(t // tr,),
        in_specs=[
            pl.BlockSpec((tr, width), lambda i: (i, 0)),
            pl.BlockSpec((tr, width), lambda i: (i, 1)),
            pl.BlockSpec((1, width), lambda i: (0, 0)),
            pl.BlockSpec((1, width), lambda i: (0, 0)),
        ],
        out_specs=pl.BlockSpec((tr, width), lambda i: (i, 0)),
        compiler_params=_params(("parallel",)),
        name="convnorm",
    )(cv, p_gate, ln_g, ln_b)


def _proj_kernel(ap_ref, ac_ref, wpp_ref, wpc_ref, bc_ref, gp_ref, gc_ref, o_ref):
    y_pool = jnp.dot(ap_ref[...], wpp_ref[...], preferred_element_type=F32)
    y_conv = jnp.dot(ac_ref[...], wpc_ref[...], preferred_element_type=F32) + bc_ref[...]
    merged = _sigmoid(gp_ref[...].astype(F32)) * y_pool + _sigmoid(gc_ref[...].astype(F32)) * y_conv
    o_ref[...] = merged.astype(o_ref.dtype)


def _proj(act_pool, act_conv, wpp, wpc, b_proj_conv, p_gate):
    t, width = act_pool.shape
    d = wpp.shape[1]
    tm, tn = _tile(t, 512), _tile(d, 512)
    nd = d // tn
    seg_gp = (2 * width) // tn
    seg_gc = seg_gp + nd
    return pl.pallas_call(
        _proj_kernel,
        out_shape=jax.ShapeDtypeStruct((t, d), BF16),
        grid=(t // tm, nd),
        in_specs=[
            pl.BlockSpec((tm, width), lambda i, j: (i, 0)),
            pl.BlockSpec((tm, width), lambda i, j: (i, 0)),
            pl.BlockSpec((width, tn), lambda i, j: (0, j)),
            pl.BlockSpec((width, tn), lambda i, j: (0, j)),
            pl.BlockSpec((1, tn), lambda i, j: (0, j)),
            pl.BlockSpec((tm, tn), lambda i, j: (i, seg_gp + j)),
            pl.BlockSpec((tm, tn), lambda i, j: (i, seg_gc + j)),
        ],
        out_specs=pl.BlockSpec((tm, tn), lambda i, j: (i, j)),
        compiler_params=_params(("parallel", "arbitrary")),
        name="proj",
    )(act_pool, act_conv, wpp, wpc, b_proj_conv, p_gate, p_gate)


def _out_kernel(m_ref, w_ref, x_ref, g_ref, o_ref, y_scr):
    nd, tm, tn = y_scr.shape
    j = pl.program_id(1)
    y_scr[j] = x_ref[...] + jnp.dot(m_ref[...], w_ref[...], preferred_element_type=F32)

    @pl.when(j == nd - 1)
    def _():
        def body(r):
            rows = pl.ds(r, 16)
            ss = jnp.zeros((16, 1), F32)
            for c in range(nd):
                v = y_scr[c, rows, :]
                ss = ss + jnp.sum(v * v, axis=-1, keepdims=True)
            inv = lax.rsqrt(ss / (nd * tn) + RMS_EPS)
            for c in range(nd):
                cols = slice(c * tn, (c + 1) * tn)
                o_ref[rows, cols] = (y_scr[c, rows, :] * inv) * g_ref[:, cols]
        _row_loop(tm, 16, body)


def _out(merged, w_out, x, g_final):
    t, d = x.shape
    tm, tn = _tile(t, 512), _tile(d, 512)
    return pl.pallas_call(
        _out_kernel,
        out_shape=jax.ShapeDtypeStruct((t, d), F32),
        grid=(t // tm, d // tn),
        in_specs=[
            pl.BlockSpec((tm, d), lambda i, j: (i, 0)),
            pl.BlockSpec((d, tn), lambda i, j: (0, j)),
            pl.BlockSpec((tm, tn), lambda i, j: (i, j)),
            pl.BlockSpec((1, d), lambda i, j: (0, 0)),
        ],
        out_specs=pl.BlockSpec((tm, d), lambda i, j: (i, 0)),
        scratch_shapes=[pltpu.VMEM((d // tn, tm, tn), F32)],
        compiler_params=_params(("parallel", "arbitrary")),
        name="outproj",
    )(merged, w_out, x, g_final)


def _layer(x, weights):
    (g_norm, w_in, w_pool, ls_pool, wpp, conv_w, conv_b, ln_g, ln_b, wpc, b_proj_conv, w_out,
     g_final) = weights
    batch, seq_len, d = x.shape
    width = wpp.shape[0]
    group_size = w_pool.shape[1]
    xt = x.reshape(batch * seq_len, d)
    p_seq, h = _inproj_seq(xt, g_norm, w_in, width)
    p_gate, cv, mixed = _inproj_gate(h, w_in, p_seq, conv_w, conv_b, seq_len=seq_len, width=width,
                                     group_size=group_size)
    act_pool = _poolmix(mixed, w_pool, p_gate, ls_pool)
    act_conv = _convnorm(cv, p_gate, ln_g, ln_b)
    merged = _proj(act_pool, act_conv, wpp, wpc, b_proj_conv, p_gate)
    return _out(merged, w_out, xt, g_final).reshape(batch, seq_len, d)


def kernel(x_prompt, x_sample, g_norm, w_in, w_pool, ls_pool, w_proj_pool, conv_w, conv_b, ln_g, ln_b,
           w_proj_conv, b_proj_conv, w_out, g_final):
    depth = w_in.shape[0]
    d = x_prompt.shape[-1]
    width = w_proj_pool.shape[1]
    assert depth == 1 and w_proj_conv.shape[1] == width == d, "segment layout assumes equal branch widths"
    assert w_in.shape[2] == 5 * width + 2 * d and conv_w.shape[1] // 2 < HALO
    assert w_pool.shape[1] == len(POOL_WINDOWS) and max(POOL_WINDOWS) // 2 <= HALO
    weights = (
        g_norm[0][None, :], w_in[0].astype(BF16), w_pool[0].astype(BF16), ls_pool[0][None, :],
        w_proj_pool[0].astype(BF16), conv_w[0], conv_b[0][None, :], ln_g[0][None, :], ln_b[0][None, :],
        w_proj_conv[0].astype(BF16), b_proj_conv[0][None, :], w_out[0].astype(BF16), g_final[None, :],
    )
    return (_layer(x_prompt, weights), _layer(x_sample, weights))
```

```python
import functools

import jax
import jax.numpy as jnp
from jax import lax
from jax.experimental import pallas as pl
from jax.experimental.pallas import tpu as pltpu

F32 = jnp.float32
BF16 = jnp.bfloat16

RMS_EPS = 1e-6
LN_EPS = 1e-5
POOL_WINDOWS = (2, 4, 8, 16)

LANES = 128
SUBLANES = 8
HALO = 16
ROW_CHUNK = 64
ROW_STRIDE = 4
CONV_CLASSES = 1
VMEM_LIMIT_BYTES = 56 * 1024 * 1024


def _params(semantics):
    return pltpu.CompilerParams(dimension_semantics=semantics, vmem_limit_bytes=VMEM_LIMIT_BYTES)


def _tile(dim, target):
    t = min(dim, target)
    while dim % t:
        t -= 1
    return t


def _sigmoid(x):
    return 0.5 * jnp.tanh(0.5 * x) + 0.5


def _silu(x):
    return x * _sigmoid(x)


def _zero_after(x):
    bits = pltpu.bitcast(x, jnp.uint32)
    return lax.shift_right_logical(lax.shift_right_logical(bits, jnp.uint32(16)), jnp.uint32(16)).astype(F32)


def _row_loop(n_rows, chunk, body, unroll=1):
    def step(c, carry):
        body(pl.multiple_of(c * chunk, chunk))
        return carry
    lax.fori_loop(0, n_rows // chunk, step, 0, unroll=unroll)


def _rmsnorm_kernel(x_ref, g_ref, h_ref):
    d = x_ref.shape[1]

    def body(r):
        rows = pl.ds(r, 16)
        sq = None
        for lb in range(d // LANES):
            x = x_ref[rows, lb * LANES:(lb + 1) * LANES]
            sq = x * x if sq is None else sq + x * x
        ms = jnp.sum(sq, axis=-1, keepdims=True) / d
        inv = jnp.broadcast_to(lax.rsqrt(ms + RMS_EPS), (16, LANES))
        for lb in range(d // LANES):
            lanes = slice(lb * LANES, (lb + 1) * LANES)
            h_ref[rows, lanes] = ((x_ref[rows, lanes] * inv) * g_ref[:, lanes]).astype(BF16)
    _row_loop(x_ref.shape[0], 16, body, unroll=2)


def _rmsnorm(x, g_norm):
    t, d = x.shape
    tr = _tile(t, 256)
    return pl.pallas_call(
        _rmsnorm_kernel,
        out_shape=jax.ShapeDtypeStruct((t, d), BF16),
        grid=(t // tr,),
        in_specs=[pl.BlockSpec((tr, d), lambda i: (i, 0)), pl.BlockSpec((1, d), lambda i: (0, 0))],
        out_specs=pl.BlockSpec((tr, d), lambda i: (i, 0)),
        compiler_params=_params(("parallel",)),
        name="rmsnorm",
    )(x, g_norm)


def _inproj_seq_kernel(h_ref, w_ref, o_ref):
    o_ref[...] = jnp.dot(h_ref[...], w_ref[...], preferred_element_type=F32).astype(o_ref.dtype)


def _inproj_seq(h, w_in, width):
    t, d = h.shape
    tm, tn = _tile(t, 1024), _tile(width, 1024)
    nseg = width // tn
    w_col = lambda i, j: (0, j + jnp.where(j >= nseg, nseg, 0))
    return pl.pallas_call(
        _inproj_seq_kernel,
        out_shape=jax.ShapeDtypeStruct((t, 3 * width), BF16),
        grid=(t // tm, 3 * nseg),
        in_specs=[
            pl.BlockSpec((tm, d), lambda i, j: (i, 0)),
            pl.BlockSpec((d, tn), w_col),
        ],
        out_specs=pl.BlockSpec((tm, tn), lambda i, j: (i, j)),
        compiler_params=_params(("parallel", "arbitrary")),
        name="inproj_seq",
    )(h, w_in)


def _pool_rings():
    rings, seen = [], set()
    for k in POOL_WINDOWS:
        offs = [d for d in range(-(k // 2), k - k // 2) if d not in seen]
        seen.update(offs)
        rings.append(offs)
    assert len(seen) == max(POOL_WINDOWS), "pooling windows must be nested"
    return rings


def _seqmix(a_ref, ap_ref, an_ref, b_ref, bp_ref, bn_ref, u_ref, up_ref, un_ref, cw_ref, cb_ref,
            cv_ref, mx_ref, v_scr, u_scr, w_scr, cvf_scr, mxf_scr, *, seq_len, group_size):
    tr, tc = a_ref.shape
    n_lb = tc // LANES
    conv_width = cw_ref.shape[0]
    conv_pad = conv_width // 2
    n_vreg = ROW_CHUNK // (SUBLANES * ROW_STRIDE)
    pos0 = (pl.program_id(0) * tr) % seq_len
    keep_prev = jnp.where(pos0 != 0, 1.0, 0.0).astype(F32)
    keep_next = jnp.where(pos0 + tr != seq_len, 1.0, 0.0).astype(F32)
    group = (pl.program_id(1) * tc) // group_size

    def glu(a, b):
        return a.astype(F32) * _sigmoid(b.astype(F32))

    for lb in range(n_lb):
        lanes = slice(lb * LANES, (lb + 1) * LANES)
        v_scr[lb, 0:HALO, :] = glu(ap_ref[:, lanes], bp_ref[:, lanes]) * keep_prev
        v_scr[lb, HALO + tr:, :] = glu(an_ref[:, lanes], bn_ref[:, lanes]) * keep_next
        u_scr[lb, 0:HALO, :] = up_ref[:, lanes].astype(F32) * keep_prev
        u_scr[lb, HALO + tr:, :] = un_ref[:, lanes].astype(F32) * keep_next
        for r in range(0, tr, ROW_CHUNK):
            rows = slice(r, r + ROW_CHUNK)
            dst = slice(HALO + r, HALO + r + ROW_CHUNK)
            v_scr[lb, dst, :] = glu(a_ref[rows, lanes], b_ref[rows, lanes])
            u_scr[lb, dst, :] = u_ref[rows, lanes].astype(F32)
    for tap in range(conv_width):
        w_scr[tap] = jnp.broadcast_to(cw_ref[tap:tap + 1, :], (SUBLANES, tc))
    w_scr[conv_width] = jnp.broadcast_to(cb_ref[...], (SUBLANES, tc))

    def strided(ref, lb, start):
        return ref[lb, pl.ds(start, SUBLANES * n_vreg, stride=ROW_STRIDE), :]

    def put(ref, lb, start, val):
        ref[lb, pl.ds(start, SUBLANES * n_vreg, stride=ROW_STRIDE), :] = val

    carry = jnp.zeros((SUBLANES, LANES), F32)
    for r in range(0, tr, ROW_CHUNK):
        for lb in range(n_lb):
            lanes = slice(lb * LANES, (lb + 1) * LANES)
            for c0 in range(0, ROW_STRIDE, CONV_CLASSES):
                classes = range(c0, c0 + CONV_CLASSES)
                acc = {c: [w_scr[conv_width, :, lanes] + carry] * n_vreg for c in classes}
                window = [strided(v_scr, lb, HALO - conv_pad + r + c0 + s) for s in range(CONV_CLASSES - 1)]
                for tap in range(conv_width):
                    window.append(strided(v_scr, lb, HALO - conv_pad + r + c0 + tap + CONV_CLASSES - 1))
                    w = w_scr[tap, :, lanes]
                    for c in classes:
                        x = window[c - c0]
                        acc[c] = [acc[c][q] + w * x[SUBLANES * q:SUBLANES * (q + 1)]
                                  for q in range(n_vreg)]
                    window.pop(0)
                for c in classes:
                    put(cvf_scr, lb, r + c, jnp.concatenate(acc[c], axis=0))
                carry = _zero_after(acc[c0][-1])

    rings = _pool_rings()
    k_half = jnp.int32(POOL_WINDOWS[-1] // 2)
    k_rest = jnp.int32(POOL_WINDOWS[-1] - POOL_WINDOWS[-1] // 2 - 1)
    for gi in range(len(POOL_WINDOWS) - 2, -1, -1):
        k = POOL_WINDOWS[gi]
        k_half = jnp.where(group == gi, k // 2, k_half)
        k_rest = jnp.where(group == gi, k - k // 2 - 1, k_rest)
    for r in range(0, tr, ROW_CHUNK):
        for c in range(ROW_STRIDE):
            row = lax.broadcasted_iota(jnp.int32, (SUBLANES * n_vreg, LANES), 0)
            pos = pos0 + (r + c) + ROW_STRIDE * row
            lo = jnp.maximum(pos - k_half, 0)
            hi = jnp.minimum(pos + k_rest, seq_len - 1)
            inv_cnt = 1.0 / (hi - lo + 1).astype(F32)
            for lb in range(n_lb):
                ring_sum, mid = [], None
                for offs in rings:
                    tot = None
                    for d in offs:
                        x = strided(u_scr, lb, HALO + r + c + d)
                        tot = x if tot is None else tot + x
                        if d == 0:
                            mid = x
                    ring_sum.append(tot)
                win = ring_sum[0]
                sel = win
                for gi in range(1, len(rings)):
                    win = win + ring_sum[gi]
                    sel = jnp.where(group >= gi, win, sel)
                mixed = (sel * inv_cnt - mid) + jnp.concatenate([carry] * n_vreg, axis=0)
                put(mxf_scr, lb, r + c, mixed)
                carry = _zero_after(mixed[:SUBLANES])

    for lb in range(n_lb):
        lanes = slice(lb * LANES, (lb + 1) * LANES)
        for r in range(0, tr, ROW_CHUNK):
            rows = slice(r, r + ROW_CHUNK)
            cv_ref[rows, lanes] = cvf_scr[lb, rows, :].astype(cv_ref.dtype)
            mx_ref[rows, lanes] = mxf_scr[lb, rows, :].astype(mx_ref.dtype)


def _inproj_gate_kernel(h_ref, w_ref, *refs, seq_len, group_size):
    seq_in, (p_ref, cv_ref, mx_ref), scratch = refs[:11], refs[11:14], refs[14:]
    p_ref[...] = jnp.dot(h_ref[...], w_ref[...], preferred_element_type=F32).astype(p_ref.dtype)
    _seqmix(*seq_in, cv_ref, mx_ref, *scratch, seq_len=seq_len, group_size=group_size)


def _inproj_gate(h, w_in, p_seq, conv_w, conv_b, *, seq_len, width, group_size):
    t, d = h.shape
    tm = _tile(seq_len, 1024)
    tn = _tile(width, 1024)
    nseg = width // tn
    n_steps = 4 * nseg
    tc = width // n_steps
    assert tc % LANES == 0 and group_size % tc == 0 and tm % ROW_CHUNK == 0
    nc = width // tc
    hb = tm // HALO
    last_hb = t // HALO - 1
    w_col = lambda i, j: (0, nseg + j + jnp.where(j >= nseg, 2 * nseg, 0))

    def main(seg):
        return pl.BlockSpec((tm, tc), lambda i, j: (i, seg * nc + j))

    def prev(seg):
        return pl.BlockSpec((HALO, tc), lambda i, j: (jnp.maximum(i * hb - 1, 0), seg * nc + j))

    def nxt(seg):
        return pl.BlockSpec((HALO, tc), lambda i, j: (jnp.minimum((i + 1) * hb, last_hb), seg * nc + j))

    seg_u, seg_a, seg_b = 0, 1, 2
    n_lb = tc // LANES
    kernel = functools.partial(_inproj_gate_kernel, seq_len=seq_len, group_size=group_size)
    return pl.pallas_call(
        kernel,
        out_shape=(jax.ShapeDtypeStruct((t, 4 * width), BF16),
                   jax.ShapeDtypeStruct((t, width), BF16),
                   jax.ShapeDtypeStruct((t, width), BF16)),
        grid=(t // tm, n_steps),
        in_specs=[pl.BlockSpec((tm, d), lambda i, j: (i, 0)),
                  pl.BlockSpec((d, tn), w_col),
                  main(seg_a), prev(seg_a), nxt(seg_a), main(seg_b), prev(seg_b), nxt(seg_b),
                  main(seg_u), prev(seg_u), nxt(seg_u),
                  pl.BlockSpec((conv_w.shape[0], tc), lambda i, j: (0, j)),
                  pl.BlockSpec((1, tc), lambda i, j: (0, j))],
        out_specs=(pl.BlockSpec((tm, tn), lambda i, j: (i, j)),
                   pl.BlockSpec((tm, tc), lambda i, j: (i, j)),
                   pl.BlockSpec((tm, tc), lambda i, j: (i, j))),
        scratch_shapes=[pltpu.VMEM((n_lb, tm + 2 * HALO, LANES), F32),
                        pltpu.VMEM((n_lb, tm + 2 * HALO, LANES), F32),
                        pltpu.VMEM((conv_w.shape[0] + 1, SUBLANES, tc), F32),
                        pltpu.VMEM((n_lb, tm, LANES), F32),
                        pltpu.VMEM((n_lb, tm, LANES), F32)],
        compiler_params=_params(("parallel", "parallel")),
        name="inproj_gate",
    )(h, w_in, p_seq, p_seq, p_seq, p_seq, p_seq, p_seq, p_seq, p_seq, p_seq, conv_w, conv_b)


def _poolmix_kernel(m_ref, w_ref, z_ref, ls_ref, o_ref):
    y = jnp.dot(m_ref[...], w_ref[0], preferred_element_type=F32)
    o_ref[...] = ((y * ls_ref[...]) * _silu(z_ref[...].astype(F32))).astype(o_ref.dtype)


def _poolmix(mixed, w_pool, p_gate, ls_pool):
    t, width = mixed.shape
    n_groups, gs, _ = w_pool.shape
    tm = _tile(t, 1024)
    return pl.pallas_call(
        _poolmix_kernel,
        out_shape=jax.ShapeDtypeStruct((t, width), BF16),
        grid=(t // tm, n_groups),
        in_specs=[
            pl.BlockSpec((tm, gs), lambda i, g: (i, g)),
            pl.BlockSpec((1, gs, gs), lambda i, g: (g, 0, 0)),
            pl.BlockSpec((tm, gs), lambda i, g: (i, g)),
            pl.BlockSpec((1, gs), lambda i, g: (0, g)),
        ],
        out_specs=pl.BlockSpec((tm, gs), lambda i, g: (i, g)),
        compiler_params=_params(("parallel", "parallel")),
        name="poolmix",
    )(mixed, w_pool, p_gate, ls_pool)


def _convnorm_kernel(cv_ref, z_ref, g_ref, b_ref, o_ref):
    width = cv_ref.shape[1]
    n_lb = width // LANES

    def body(r):
        rows = pl.ds(r, 16)

        def block(lb):
            return cv_ref[rows, lb * LANES:(lb + 1) * LANES].astype(F32)

        tot = block(0)
        for lb in range(1, n_lb):
            tot = tot + block(lb)
        mu = jnp.broadcast_to(jnp.sum(tot, axis=-1, keepdims=True) / width, (16, LANES))
        sq = None
        for lb in range(n_lb):
            xc = block(lb) - mu
            sq = xc * xc if sq is None else sq + xc * xc
        var = jnp.sum(sq, axis=-1, keepdims=True) / width
        rstd = jnp.broadcast_to(lax.rsqrt(var + LN_EPS), (16, LANES))
        for lb in range(n_lb):
            lanes = slice(lb * LANES, (lb + 1) * LANES)
            y = ((block(lb) - mu) * rstd) * g_ref[:, lanes] + b_ref[:, lanes]
            z = z_ref[rows, lanes].astype(F32)
            o_ref[rows, lanes] = (_silu(y) * _silu(z)).astype(o_ref.dtype)
    _row_loop(cv_ref.shape[0], 16, body, unroll=2)


def _convnorm(cv, p_gate, ln_g, ln_b):
    t, width = cv.shape
    tr = _tile(t, 256)
    return pl.pallas_call(
        _convnorm_kernel,
        out_shape=jax.ShapeDtypeStruct((t, width), BF16),
        grid=(t // tr,),
        in_specs=[
            pl.BlockSpec((tr, width), lambda i: (i, 0)),
            pl.BlockSpec((tr, width), lambda i: (i, 1)),
            pl.BlockSpec((1, width), lambda i: (0, 0)),
            pl.BlockSpec((1, width), lambda i: (0, 0)),
        ],
        out_specs=pl.BlockSpec((tr, width), lambda i: (i, 0)),
        compiler_params=_params(("parallel",)),
        name="convnorm",
    )(cv, p_gate, ln_g, ln_b)


def _proj_kernel(ap_ref, ac_ref, wpp_ref, wpc_ref, bc_ref, gp_ref, gc_ref, o_ref):
    y_pool = jnp.dot(ap_ref[...], wpp_ref[...], preferred_element_type=F32)
    y_conv = jnp.dot(ac_ref[...], wpc_ref[...], preferred_element_type=F32) + bc_ref[...]
    merged = _sigmoid(gp_ref[...].astype(F32)) * y_pool + _sigmoid(gc_ref[...].astype(F32)) * y_conv
    o_ref[...] = merged.astype(o_ref.dtype)


def _proj(act_pool, act_conv, wpp, wpc, b_proj_conv, p_gate):
    t, width = act_pool.shape
    d = wpp.shape[1]
    tm, tn = _tile(t, 512), _tile(d, 512)
    nd = d // tn
    seg_gp = (2 * width) // tn
    seg_gc = seg_gp + nd
    return pl.pallas_call(
        _proj_kernel,
        out_shape=jax.ShapeDtypeStruct((t, d), BF16),
        grid=(t // tm, nd),
        in_specs=[
            pl.BlockSpec((tm, width), lambda i, j: (i, 0)),
            pl.BlockSpec((tm, width), lambda i, j: (i, 0)),
            pl.BlockSpec((width, tn), lambda i, j: (0, j)),
            pl.BlockSpec((width, tn), lambda i, j: (0, j)),
            pl.BlockSpec((1, tn), lambda i, j: (0, j)),
            pl.BlockSpec((tm, tn), lambda i, j: (i, seg_gp + j)),
            pl.BlockSpec((tm, tn), lambda i, j: (i, seg_gc + j)),
        ],
        out_specs=pl.BlockSpec((tm, tn), lambda i, j: (i, j)),
        compiler_params=_params(("parallel", "arbitrary")),
        name="proj",
    )(act_pool, act_conv, wpp, wpc, b_proj_conv, p_gate, p_gate)


def _out_kernel(m_ref, w_ref, x_ref, g_ref, o_ref, y_scr):
    nd, tm, tn = y_scr.shape
    j = pl.program_id(1)
    y_scr[j] = x_ref[...] + jnp.dot(m_ref[...], w_ref[...], preferred_element_type=F32)

    @pl.when(j == nd - 1)
    def _():
        def body(r):
            rows = pl.ds(r, 16)
            ss = jnp.zeros((16, 1), F32)
            for c in range(nd):
                v = y_scr[c, rows, :]
                ss = ss + jnp.sum(v * v, axis=-1, keepdims=True)
            inv = lax.rsqrt(ss / (nd * tn) + RMS_EPS)
            for c in range(nd):
                cols = slice(c * tn, (c + 1) * tn)
                o_ref[rows, cols] = (y_scr[c, rows, :] * inv) * g_ref[:, cols]
        _row_loop(tm, 16, body, unroll=2)


def _out(merged, w_out, x, g_final):
    t, d = x.shape
    tm, tn = _tile(t, 512), _tile(d, 512)
    return pl.pallas_call(
        _out_kernel,
        out_shape=jax.ShapeDtypeStruct((t, d), F32),
        grid=(t // tm, d // tn),
        in_specs=[
            pl.BlockSpec((tm, d), lambda i, j: (i, 0)),
            pl.BlockSpec((d, tn), lambda i, j: (0, j)),
            pl.BlockSpec((tm, tn), lambda i, j: (i, j)),
            pl.BlockSpec((1, d), lambda i, j: (0, 0)),
        ],
        out_specs=pl.BlockSpec((tm, d), lambda i, j: (i, 0)),
        scratch_shapes=[pltpu.VMEM((d // tn, tm, tn), F32)],
        compiler_params=_params(("parallel", "arbitrary")),
        name="outproj",
    )(merged, w_out, x, g_final)


def _layer(x, weights):
    (g_norm, w_in, w_pool, ls_pool, wpp, conv_w, conv_b, ln_g, ln_b, wpc, b_proj_conv, w_out,
     g_final) = weights
    batch, seq_len, d = x.shape
    width = wpp.shape[0]
    group_size = w_pool.shape[1]
    xt = x.reshape(batch * seq_len, d)
    h = _rmsnorm(xt, g_norm)
    p_seq = _inproj_seq(h, w_in, width)
    p_gate, cv, mixed = _inproj_gate(h, w_in, p_seq, conv_w, conv_b, seq_len=seq_len, width=width,
                                     group_size=group_size)
    act_pool = _poolmix(mixed, w_pool, p_gate, ls_pool)
    act_conv = _convnorm(cv, p_gate, ln_g, ln_b)
    merged = _proj(act_pool, act_conv, wpp, wpc, b_proj_conv, p_gate)
    return _out(merged, w_out, xt, g_final).reshape(batch, seq_len, d)


def kernel(x_prompt, x_sample, g_norm, w_in, w_pool, ls_pool, w_proj_pool, conv_w, conv_b, ln_g, ln_b,
           w_proj_conv, b_proj_conv, w_out, g_final):
    depth = w_in.shape[0]
    d = x_prompt.shape[-1]
    width = w_proj_pool.shape[1]
    assert depth == 1 and w_proj_conv.shape[1] == width == d, "segment layout assumes equal branch widths"
    assert w_in.shape[2] == 5 * width + 2 * d and conv_w.shape[1] // 2 < HALO
    assert w_pool.shape[1] == len(POOL_WINDOWS) and max(POOL_WINDOWS) // 2 <= HALO
    weights = (
        g_norm[0][None, :], w_in[0].astype(BF16), w_pool[0].astype(BF16), ls_pool[0][None, :],
        w_proj_pool[0].astype(BF16), conv_w[0], conv_b[0][None, :], ln_g[0][None, :], ln_b[0][None, :],
        w_proj_conv[0].astype(BF16), b_proj_conv[0][None, :], w_out[0].astype(BF16), g_final[None, :],
    )
    return (_layer(x_prompt, weights), _layer(x_sample, weights))
```

```python
import functools

import jax
import jax.numpy as jnp
from jax import lax
from jax.experimental import pallas as pl
from jax.experimental.pallas import tpu as pltpu

F32 = jnp.float32
BF16 = jnp.bfloat16

RMS_EPS = 1e-6
LN_EPS = 1e-5
POOL_WINDOWS = (2, 4, 8, 16)

LANES = 128
SUBLANES = 8
HALO = 16
ROW_CHUNK = 64
ROW_STRIDE = 4
CONV_CLASSES = 1
VMEM_LIMIT_BYTES = 56 * 1024 * 1024


def _params(semantics):
    return pltpu.CompilerParams(dimension_semantics=semantics, vmem_limit_bytes=VMEM_LIMIT_BYTES)


def _tile(dim, target):
    t = min(dim, target)
    while dim % t:
        t -= 1
    return t


def _sigmoid(x):
    return 0.5 * jnp.tanh(0.5 * x) + 0.5


def _silu(x):
    return x * _sigmoid(x)


def _zero_after(x):
    bits = pltpu.bitcast(x, jnp.uint32)
    return lax.shift_right_logical(lax.shift_right_logical(bits, jnp.uint32(16)), jnp.uint32(16)).astype(F32)


def _row_loop(n_rows, chunk, body, unroll=1):
    def step(c, carry):
        body(pl.multiple_of(c * chunk, chunk))
        return carry
    lax.fori_loop(0, n_rows // chunk, step, 0, unroll=unroll)


def _rmsnorm_kernel(x_ref, g_ref, h_ref):
    d = x_ref.shape[1]

    def body(r):
        rows = pl.ds(r, 16)
        sq = None
        for lb in range(d // LANES):
            x = x_ref[rows, lb * LANES:(lb + 1) * LANES]
            sq = x * x if sq is None else sq + x * x
        ms = jnp.sum(sq, axis=-1, keepdims=True) / d
        inv = jnp.broadcast_to(lax.rsqrt(ms + RMS_EPS), (16, LANES))
        for lb in range(d // LANES):
            lanes = slice(lb * LANES, (lb + 1) * LANES)
            h_ref[rows, lanes] = ((x_ref[rows, lanes] * inv) * g_ref[:, lanes]).astype(BF16)
    _row_loop(x_ref.shape[0], 16, body, unroll=2)


def _rmsnorm(x, g_norm):
    t, d = x.shape
    tr = _tile(t, 256)
    return pl.pallas_call(
        _rmsnorm_kernel,
        out_shape=jax.ShapeDtypeStruct((t, d), BF16),
        grid=(t // tr,),
        in_specs=[pl.BlockSpec((tr, d), lambda i: (i, 0)), pl.BlockSpec((1, d), lambda i: (0, 0))],
        out_specs=pl.BlockSpec((tr, d), lambda i: (i, 0)),
        compiler_params=_params(("parallel",)),
        name="rmsnorm",
    )(x, g_norm)


def _inproj_seq_kernel(h_ref, w_ref, o_ref):
    o_ref[...] = jnp.dot(h_ref[...], w_ref[...], preferred_element_type=F32).astype(o_ref.dtype)


def _inproj_seq(h, w_in, width):
    t, d = h.shape
    tm, tn = _tile(t, 1024), _tile(width, 1024)
    nseg = width // tn
    w_col = lambda i, j: (0, j + jnp.where(j >= nseg, nseg, 0))
    return pl.pallas_call(
        _inproj_seq_kernel,
        out_shape=jax.ShapeDtypeStruct((t, 3 * width), BF16),
        grid=(t // tm, 3 * nseg),
        in_specs=[
            pl.BlockSpec((tm, d), lambda i, j: (i, 0)),
            pl.BlockSpec((d, tn), w_col),
        ],
        out_specs=pl.BlockSpec((tm, tn), lambda i, j: (i, j)),
        compiler_params=_params(("parallel", "arbitrary")),
        name="inproj_seq",
    )(h, w_in)


def _seqmix(a_ref, ap_ref, an_ref, b_ref, bp_ref, bn_ref, u_ref, up_ref, un_ref, cw_ref, cb_ref,
            cv_ref, mx_ref, v_scr, u_scr, w_scr, cvf_scr, mxf_scr, *, seq_len, group_size):
    tr, tc = a_ref.shape
    n_lb = tc // LANES
    conv_width = cw_ref.shape[0]
    conv_pad = conv_width // 2
    n_vreg = ROW_CHUNK // (SUBLANES * ROW_STRIDE)
    pos0 = (pl.program_id(0) * tr) % seq_len
    keep_prev = jnp.where(pos0 != 0, 1.0, 0.0).astype(F32)
    keep_next = jnp.where(pos0 + tr != seq_len, 1.0, 0.0).astype(F32)
    group = (pl.program_id(1) * tc) // group_size

    def glu(a, b):
        return a.astype(F32) * _sigmoid(b.astype(F32))

    for lb in range(n_lb):
        lanes = slice(lb * LANES, (lb + 1) * LANES)
        v_scr[lb, 0:HALO, :] = glu(ap_ref[:, lanes], bp_ref[:, lanes]) * keep_prev
        v_scr[lb, HALO + tr:, :] = glu(an_ref[:, lanes], bn_ref[:, lanes]) * keep_next
        u_scr[lb, 0:HALO, :] = up_ref[:, lanes].astype(F32) * keep_prev
        u_scr[lb, HALO + tr:, :] = un_ref[:, lanes].astype(F32) * keep_next
        for r in range(0, tr, ROW_CHUNK):
            rows = slice(r, r + ROW_CHUNK)
            dst = slice(HALO + r, HALO + r + ROW_CHUNK)
            v_scr[lb, dst, :] = glu(a_ref[rows, lanes], b_ref[rows, lanes])
            u_scr[lb, dst, :] = u_ref[rows, lanes].astype(F32)
    for tap in range(conv_width):
        w_scr[tap] = jnp.broadcast_to(cw_ref[tap:tap + 1, :], (SUBLANES, tc))
    w_scr[conv_width] = jnp.broadcast_to(cb_ref[...], (SUBLANES, tc))

    def strided(ref, lb, start):
        return ref[lb, pl.ds(start, SUBLANES * n_vreg, stride=ROW_STRIDE), :]

    def put(ref, lb, start, val):
        ref[lb, pl.ds(start, SUBLANES * n_vreg, stride=ROW_STRIDE), :] = val

    carry = jnp.zeros((SUBLANES, LANES), F32)
    for r in range(0, tr, ROW_CHUNK):
        for lb in range(n_lb):
            lanes = slice(lb * LANES, (lb + 1) * LANES)
            for c0 in range(0, ROW_STRIDE, CONV_CLASSES):
                classes = range(c0, c0 + CONV_CLASSES)
                acc = {c: [w_scr[conv_width, :, lanes] + carry] * n_vreg for c in classes}
                window = [strided(v_scr, lb, HALO - conv_pad + r + c0 + s) for s in range(CONV_CLASSES - 1)]
                for tap in range(conv_width):
                    window.append(strided(v_scr, lb, HALO - conv_pad + r + c0 + tap + CONV_CLASSES - 1))
                    w = w_scr[tap, :, lanes]
                    for c in classes:
                        x = window[c - c0]
                        acc[c] = [acc[c][q] + w * x[SUBLANES * q:SUBLANES * (q + 1)]
                                  for q in range(n_vreg)]
                    window.pop(0)
                for c in classes:
                    put(cvf_scr, lb, r + c, jnp.concatenate(acc[c], axis=0))
                carry = _zero_after(acc[c0][-1])

    for gi, k in enumerate(POOL_WINDOWS):
        @pl.when(group == gi)
        def _(k=k):
            for r in range(0, tr, ROW_CHUNK):
                for c in range(ROW_STRIDE):
                    row = lax.broadcasted_iota(jnp.int32, (SUBLANES * n_vreg, LANES), 0)
                    pos = pos0 + (r + c) + ROW_STRIDE * row
                    lo = jnp.maximum(pos - k // 2, 0)
                    hi = jnp.minimum(pos + (k - k // 2 - 1), seq_len - 1)
                    inv_cnt = 1.0 / (hi - lo + 1).astype(F32)
                    for lb in range(n_lb):
                        tot, mid = None, None
                        for d in range(-(k // 2), k - k // 2):
                            x = strided(u_scr, lb, HALO + r + c + d)
                            tot = x if tot is None else tot + x
                            if d == 0:
                                mid = x
                        put(mxf_scr, lb, r + c, tot * inv_cnt - mid)

    for lb in range(n_lb):
        lanes = slice(lb * LANES, (lb + 1) * LANES)
        for r in range(0, tr, ROW_CHUNK):
            rows = slice(r, r + ROW_CHUNK)
            cv_ref[rows, lanes] = cvf_scr[lb, rows, :].astype(cv_ref.dtype)
            mx_ref[rows, lanes] = mxf_scr[lb, rows, :].astype(mx_ref.dtype)


def _inproj_gate_kernel(h_ref, w_ref, *refs, seq_len, group_size):
    seq_in, (p_ref, cv_ref, mx_ref), scratch = refs[:11], refs[11:14], refs[14:]
    p_ref[...] = jnp.dot(h_ref[...], w_ref[...], preferred_element_type=F32).astype(p_ref.dtype)
    _seqmix(*seq_in, cv_ref, mx_ref, *scratch, seq_len=seq_len, group_size=group_size)


def _inproj_gate(h, w_in, p_seq, conv_w, conv_b, *, seq_len, width, group_size):
    t, d = h.shape
    tm = _tile(seq_len, 1024)
    tn = _tile(width, 1024)
    nseg = width // tn
    n_steps = 4 * nseg
    tc = width // n_steps
    assert tc % LANES == 0 and group_size % tc == 0 and tm % ROW_CHUNK == 0
    nc = width // tc
    hb = tm // HALO
    last_hb = t // HALO - 1
    w_col = lambda i, j: (0, nseg + j + jnp.where(j >= nseg, 2 * nseg, 0))

    def main(seg):
        return pl.BlockSpec((tm, tc), lambda i, j: (i, seg * nc + j))

    def prev(seg):
        return pl.BlockSpec((HALO, tc), lambda i, j: (jnp.maximum(i * hb - 1, 0), seg * nc + j))

    def nxt(seg):
        return pl.BlockSpec((HALO, tc), lambda i, j: (jnp.minimum((i + 1) * hb, last_hb), seg * nc + j))

    seg_u, seg_a, seg_b = 0, 1, 2
    n_lb = tc // LANES
    kernel = functools.partial(_inproj_gate_kernel, seq_len=seq_len, group_size=group_size)
    return pl.pallas_call(
        kernel,
        out_shape=(jax.ShapeDtypeStruct((t, 4 * width), BF16),
                   jax.ShapeDtypeStruct((t, width), BF16),
                   jax.ShapeDtypeStruct((t, width), BF16)),
        grid=(t // tm, n_steps),
        in_specs=[pl.BlockSpec((tm, d), lambda i, j: (i, 0)),
                  pl.BlockSpec((d, tn), w_col),
                  main(seg_a), prev(seg_a), nxt(seg_a), main(seg_b), prev(seg_b), nxt(seg_b),
                  main(seg_u), prev(seg_u), nxt(seg_u),
                  pl.BlockSpec((conv_w.shape[0], tc), lambda i, j: (0, j)),
                  pl.BlockSpec((1, tc), lambda i, j: (0, j))],
        out_specs=(pl.BlockSpec((tm, tn), lambda i, j: (i, j)),
                   pl.BlockSpec((tm, tc), lambda i, j: (i, j)),
                   pl.BlockSpec((tm, tc), lambda i, j: (i, j))),
        scratch_shapes=[pltpu.VMEM((n_lb, tm + 2 * HALO, LANES), F32),
                        pltpu.VMEM((n_lb, tm + 2 * HALO, LANES), F32),
                        pltpu.VMEM((conv_w.shape[0] + 1, SUBLANES, tc), F32),
                        pltpu.VMEM((n_lb, tm, LANES), F32),
                        pltpu.VMEM((n_lb, tm, LANES), F32)],
        compiler_params=_params(("parallel", "parallel")),
        name="inproj_gate",
    )(h, w_in, p_seq, p_seq, p_seq, p_seq, p_seq, p_seq, p_seq, p_seq, p_seq, conv_w, conv_b)


def _poolmix_kernel(m_ref, w_ref, z_ref, ls_ref, o_ref):
    y = jnp.dot(m_ref[...], w_ref[0], preferred_element_type=F32)
    o_ref[...] = ((y * ls_ref[...]) * _silu(z_ref[...].astype(F32))).astype(o_ref.dtype)


def _poolmix(mixed, w_pool, p_gate, ls_pool):
    t, width = mixed.shape
    n_groups, gs, _ = w_pool.shape
    tm = _tile(t, 1024)
    return pl.pallas_call(
        _poolmix_kernel,
        out_shape=jax.ShapeDtypeStruct((t, width), BF16),
        grid=(t // tm, n_groups),
        in_specs=[
            pl.BlockSpec((tm, gs), lambda i, g: (i, g)),
            pl.BlockSpec((1, gs, gs), lambda i, g: (g, 0, 0)),
            pl.BlockSpec((tm, gs), lambda i, g: (i, g)),
            pl.BlockSpec((1, gs), lambda i, g: (0, g)),
        ],
        out_specs=pl.BlockSpec((tm, gs), lambda i, g: (i, g)),
        compiler_params=_params(("parallel", "parallel")),
        name="poolmix",
    )(mixed, w_pool, p_gate, ls_pool)


def _convnorm_kernel(cv_ref, z_ref, g_ref, b_ref, o_ref):
    width = cv_ref.shape[1]
    n_lb = width // LANES

    def body(r):
        rows = pl.ds(r, 16)

        def block(lb):
            return cv_ref[rows, lb * LANES:(lb + 1) * LANES].astype(F32)

        tot = block(0)
        for lb in range(1, n_lb):
            tot = tot + block(lb)
        mu = jnp.broadcast_to(jnp.sum(tot, axis=-1, keepdims=True) / width, (16, LANES))
        sq = None
        for lb in range(n_lb):
            xc = block(lb) - mu
            sq = xc * xc if sq is None else sq + xc * xc
        var = jnp.sum(sq, axis=-1, keepdims=True) / width
        rstd = jnp.broadcast_to(lax.rsqrt(var + LN_EPS), (16, LANES))
        for lb in range(n_lb):
            lanes = slice(lb * LANES, (lb + 1) * LANES)
            y = ((block(lb) - mu) * rstd) * g_ref[:, lanes] + b_ref[:, lanes]
            z = z_ref[rows, lanes].astype(F32)
            o_ref[rows, lanes] = (_silu(y) * _silu(z)).astype(o_ref.dtype)
    _row_loop(cv_ref.shape[0], 16, body, unroll=4)


def _convnorm(cv, p_gate, ln_g, ln_b):
    t, width = cv.shape
    tr = _tile(t, 256)
    return pl.pallas_call(
        _convnorm_kernel,
        out_shape=jax.ShapeDtypeStruct((t, width), BF16),
        grid=(t // tr,),
        in_specs=[
            pl.BlockSpec((tr, width), lambda i: (i, 0)),
            pl.BlockSpec((tr, width), lambda i: (i, 1)),
            pl.BlockSpec((1, width), lambda i: (0, 0)),
            pl.BlockSpec((1, width), lambda i: (0, 0)),
        ],
        out_specs=pl.BlockSpec((tr, width), lambda i: (i, 0)),
        compiler_params=_params(("parallel",)),
        name="convnorm",
    )(cv, p_gate, ln_g, ln_b)


def _proj_kernel(ap_ref, ac_ref, wpp_ref, wpc_ref, bc_ref, gp_ref, gc_ref, o_ref):
    y_pool = jnp.dot(ap_ref[...], wpp_ref[...], preferred_element_type=F32)
    y_conv = jnp.dot(ac_ref[...], wpc_ref[...], preferred_element_type=F32) + bc_ref[...]
    merged = _sigmoid(gp_ref[...].astype(F32)) * y_pool + _sigmoid(gc_ref[...].astype(F32)) * y_conv
    o_ref[...] = merged.astype(o_ref.dtype)


def _proj(act_pool, act_conv, wpp, wpc, b_proj_conv, p_gate):
    t, width = act_pool.shape
    d = wpp.shape[1]
    tm, tn = _tile(t, 512), _tile(d, 512)
    nd = d // tn
    seg_gp = (2 * width) // tn
    seg_gc = seg_gp + nd
    return pl.pallas_call(
        _proj_kernel,
        out_shape=jax.ShapeDtypeStruct((t, d), BF16),
        grid=(t // tm, nd),
        in_specs=[
            pl.BlockSpec((tm, width), lambda i, j: (i, 0)),
            pl.BlockSpec((tm, width), lambda i, j: (i, 0)),
            pl.BlockSpec((width, tn), lambda i, j: (0, j)),
            pl.BlockSpec((width, tn), lambda i, j: (0, j)),
            pl.BlockSpec((1, tn), lambda i, j: (0, j)),
            pl.BlockSpec((tm, tn), lambda i, j: (i, seg_gp + j)),
            pl.BlockSpec((tm, tn), lambda i, j: (i, seg_gc + j)),
        ],
        out_specs=pl.BlockSpec((tm, tn), lambda i, j: (i, j)),
        compiler_params=_params(("parallel", "arbitrary")),
        name="proj",
    )(act_pool, act_conv, wpp, wpc, b_proj_conv, p_gate, p_gate)


def _out_kernel(m_ref, w_ref, x_ref, g_ref, o_ref, y_scr):
    nd, tm, tn = y_scr.shape
    j = pl.program_id(1)
    y_scr[j] = x_ref[...] + jnp.dot(m_ref[...], w_ref[...], preferred_element_type=F32)

    @pl.when(j == nd - 1)
    def _():
        def body(r):
            rows = pl.ds(r, 16)
            ss = jnp.zeros((16, 1), F32)
            for c in range(nd):
                v = y_scr[c, rows, :]
                ss = ss + jnp.sum(v * v, axis=-1, keepdims=True)
            inv = lax.rsqrt(ss / (nd * tn) + RMS_EPS)
            for c in range(nd):
                cols = slice(c * tn, (c + 1) * tn)
                o_ref[rows, cols] = (y_scr[c, rows, :] * inv) * g_ref[:, cols]
        _row_loop(tm, 16, body, unroll=2)


def _out(merged, w_out, x, g_final):
    t, d = x.shape
    tm, tn = _tile(t, 512), _tile(d, 512)
    return pl.pallas_call(
        _out_kernel,
        out_shape=jax.ShapeDtypeStruct((t, d), F32),
        grid=(t // tm, d // tn),
        in_specs=[
            pl.BlockSpec((tm, d), lambda i, j: (i, 0)),
            pl.BlockSpec((d, tn), lambda i, j: (0, j)),
            pl.BlockSpec((tm, tn), lambda i, j: (i, j)),
            pl.BlockSpec((1, d), lambda i, j: (0, 0)),
        ],
        out_specs=pl.BlockSpec((tm, d), lambda i, j: (i, 0)),
        scratch_shapes=[pltpu.VMEM((d // tn, tm, tn), F32)],
        compiler_params=_params(("parallel", "arbitrary")),
        name="outproj",
    )(merged, w_out, x, g_final)


def _layer(x, weights):
    (g_norm, w_in, w_pool, ls_pool, wpp, conv_w, conv_b, ln_g, ln_b, wpc, b_proj_conv, w_out,
     g_final) = weights
    batch, seq_len, d = x.shape
    width = wpp.shape[0]
    group_size = w_pool.shape[1]
    xt = x.reshape(batch * seq_len, d)
    h = _rmsnorm(xt, g_norm)
    p_seq = _inproj_seq(h, w_in, width)
    p_gate, cv, mixed = _inproj_gate(h, w_in, p_seq, conv_w, conv_b, seq_len=seq_len, width=width,
                                     group_size=group_size)
    act_pool = _poolmix(mixed, w_pool, p_gate, ls_pool)
    act_conv = _convnorm(cv, p_gate, ln_g, ln_b)
    merged = _proj(act_pool, act_conv, wpp, wpc, b_proj_conv, p_gate)
    return _out(merged, w_out, xt, g_final).reshape(batch, seq_len, d)


def kernel(x_prompt, x_sample, g_norm, w_in, w_pool, ls_pool, w_proj_pool, conv_w, conv_b, ln_g, ln_b,
           w_proj_conv, b_proj_conv, w_out, g_final):
    depth = w_in.shape[0]
    d = x_prompt.shape[-1]
    width = w_proj_pool.shape[1]
    assert depth == 1 and w_proj_conv.shape[1] == width == d, "segment layout assumes equal branch widths"
    assert w_in.shape[2] == 5 * width + 2 * d and conv_w.shape[1] // 2 < HALO
    assert w_pool.shape[1] == len(POOL_WINDOWS) and max(POOL_WINDOWS) // 2 <= HALO
    weights = (
        g_norm[0][None, :], w_in[0].astype(BF16), w_pool[0].astype(BF16), ls_pool[0][None, :],
        w_proj_pool[0].astype(BF16), conv_w[0], conv_b[0][None, :], ln_g[0][None, :], ln_b[0][None, :],
        w_proj_conv[0].astype(BF16), b_proj_conv[0][None, :], w_out[0].astype(BF16), g_final[None, :],
    )
    return (_layer(x_prompt, weights), _layer(x_sample, weights))
```

```python
import functools

import jax
import jax.numpy as jnp
from jax import lax
from jax.experimental import pallas as pl
from jax.experimental.pallas import tpu as pltpu

F32 = jnp.float32
BF16 = jnp.bfloat16

RMS_EPS = 1e-6
LN_EPS = 1e-5
POOL_WINDOWS = (2, 4, 8, 16)

LANES = 128
SUBLANES = 8
HALO = 16
ROW_CHUNK = 64
ROW_STRIDE = 4
CONV_CLASSES = 1
VMEM_LIMIT_BYTES = 56 * 1024 * 1024
OUT_VMEM_LIMIT_BYTES = 60 * 1024 * 1024


def _params(semantics, vmem_limit_bytes=VMEM_LIMIT_BYTES):
    return pltpu.CompilerParams(dimension_semantics=semantics, vmem_limit_bytes=vmem_limit_bytes)


def _tile(dim, target):
    t = min(dim, target)
    while dim % t:
        t -= 1
    return t


def _sigmoid(x):
    return 0.5 * jnp.tanh(0.5 * x) + 0.5


def _silu(x):
    return x * _sigmoid(x)


def _zero_after(x):
    bits = pltpu.bitcast(x, jnp.uint32)
    return lax.shift_right_logical(lax.shift_right_logical(bits, jnp.uint32(16)), jnp.uint32(16)).astype(F32)


def _row_loop(n_rows, chunk, body, unroll=1):
    def step(c, carry):
        body(pl.multiple_of(c * chunk, chunk))
        return carry
    lax.fori_loop(0, n_rows // chunk, step, 0, unroll=unroll)


def _rmsnorm_kernel(x_ref, g_ref, h_ref):
    d = x_ref.shape[1]

    def body(r):
        rows = pl.ds(r, 16)
        sq = None
        for lb in range(d // LANES):
            x = x_ref[rows, lb * LANES:(lb + 1) * LANES]
            sq = x * x if sq is None else sq + x * x
        ms = jnp.sum(sq, axis=-1, keepdims=True) / d
        inv = jnp.broadcast_to(lax.rsqrt(ms + RMS_EPS), (16, LANES))
        for lb in range(d // LANES):
            lanes = slice(lb * LANES, (lb + 1) * LANES)
            h_ref[rows, lanes] = ((x_ref[rows, lanes] * inv) * g_ref[:, lanes]).astype(BF16)
    _row_loop(x_ref.shape[0], 16, body, unroll=2)


def _rmsnorm(x, g_norm):
    t, d = x.shape
    tr = _tile(t, 256)
    return pl.pallas_call(
        _rmsnorm_kernel,
        out_shape=jax.ShapeDtypeStruct((t, d), BF16),
        grid=(t // tr,),
        in_specs=[pl.BlockSpec((tr, d), lambda i: (i, 0)), pl.BlockSpec((1, d), lambda i: (0, 0))],
        out_specs=pl.BlockSpec((tr, d), lambda i: (i, 0)),
        compiler_params=_params(("parallel",)),
        name="rmsnorm",
    )(x, g_norm)


def _inproj_seq_kernel(h_ref, w_ref, o_ref):
    o_ref[...] = jnp.dot(h_ref[...], w_ref[...], preferred_element_type=F32).astype(o_ref.dtype)


def _inproj_seq(h, w_in, width):
    t, d = h.shape
    tm, tn = _tile(t, 1024), _tile(width, 1024)
    nseg = width // tn
    w_col = lambda i, j: (0, j + jnp.where(j >= nseg, nseg, 0))
    return pl.pallas_call(
        _inproj_seq_kernel,
        out_shape=jax.ShapeDtypeStruct((t, 3 * width), BF16),
        grid=(t // tm, 3 * nseg),
        in_specs=[
            pl.BlockSpec((tm, d), lambda i, j: (i, 0)),
            pl.BlockSpec((d, tn), w_col),
        ],
        out_specs=pl.BlockSpec((tm, tn), lambda i, j: (i, j)),
        compiler_params=_params(("parallel", "arbitrary")),
        name="inproj_seq",
    )(h, w_in)


def _seqmix(a_ref, ap_ref, an_ref, b_ref, bp_ref, bn_ref, u_ref, up_ref, un_ref, cw_ref, cb_ref,
            cv_ref, mx_ref, v_scr, u_scr, w_scr, cvf_scr, mxf_scr, *, seq_len, group_size):
    tr, tc = a_ref.shape
    n_lb = tc // LANES
    conv_width = cw_ref.shape[0]
    conv_pad = conv_width // 2
    n_vreg = ROW_CHUNK // (SUBLANES * ROW_STRIDE)
    pos0 = (pl.program_id(0) * tr) % seq_len
    keep_prev = jnp.where(pos0 != 0, 1.0, 0.0).astype(F32)
    keep_next = jnp.where(pos0 + tr != seq_len, 1.0, 0.0).astype(F32)
    group = (pl.program_id(1) * tc) // group_size

    def glu(a, b):
        return a.astype(F32) * _sigmoid(b.astype(F32))

    for lb in range(n_lb):
        lanes = slice(lb * LANES, (lb + 1) * LANES)
        v_scr[lb, 0:HALO, :] = glu(ap_ref[:, lanes], bp_ref[:, lanes]) * keep_prev
        v_scr[lb, HALO + tr:, :] = glu(an_ref[:, lanes], bn_ref[:, lanes]) * keep_next
        u_scr[lb, 0:HALO, :] = up_ref[:, lanes].astype(F32) * keep_prev
        u_scr[lb, HALO + tr:, :] = un_ref[:, lanes].astype(F32) * keep_next
        for r in range(0, tr, ROW_CHUNK):
            rows = slice(r, r + ROW_CHUNK)
            dst = slice(HALO + r, HALO + r + ROW_CHUNK)
            v_scr[lb, dst, :] = glu(a_ref[rows, lanes], b_ref[rows, lanes])
            u_scr[lb, dst, :] = u_ref[rows, lanes].astype(F32)
    for tap in range(conv_width):
        w_scr[tap] = jnp.broadcast_to(cw_ref[tap:tap + 1, :], (SUBLANES, tc))
    w_scr[conv_width] = jnp.broadcast_to(cb_ref[...], (SUBLANES, tc))

    def strided(ref, lb, start):
        return ref[lb, pl.ds(start, SUBLANES * n_vreg, stride=ROW_STRIDE), :]

    def put(ref, lb, start, val):
        ref[lb, pl.ds(start, SUBLANES * n_vreg, stride=ROW_STRIDE), :] = val

    carry = jnp.zeros((SUBLANES, LANES), F32)
    for r in range(0, tr, ROW_CHUNK):
        for lb in range(n_lb):
            lanes = slice(lb * LANES, (lb + 1) * LANES)
            for c0 in range(0, ROW_STRIDE, CONV_CLASSES):
                classes = range(c0, c0 + CONV_CLASSES)
                acc = {c: [w_scr[conv_width, :, lanes] + carry] * n_vreg for c in classes}
                window = [strided(v_scr, lb, HALO - conv_pad + r + c0 + s) for s in range(CONV_CLASSES - 1)]
                for tap in range(conv_width):
                    window.append(strided(v_scr, lb, HALO - conv_pad + r + c0 + tap + CONV_CLASSES - 1))
                    w = w_scr[tap, :, lanes]
                    for c in classes:
                        x = window[c - c0]
                        acc[c] = [acc[c][q] + w * x[SUBLANES * q:SUBLANES * (q + 1)]
                                  for q in range(n_vreg)]
                    window.pop(0)
                for c in classes:
                    put(cvf_scr, lb, r + c, jnp.concatenate(acc[c], axis=0))
                carry = _zero_after(acc[c0][-1])

    for gi, k in enumerate(POOL_WINDOWS):
        @pl.when(group == gi)
        def _(k=k):
            for r in range(0, tr, ROW_CHUNK):
                for c in range(ROW_STRIDE):
                    row = lax.broadcasted_iota(jnp.int32, (SUBLANES * n_vreg, LANES), 0)
                    pos = pos0 + (r + c) + ROW_STRIDE * row
                    lo = jnp.maximum(pos - k // 2, 0)
                    hi = jnp.minimum(pos + (k - k // 2 - 1), seq_len - 1)
                    inv_cnt = 1.0 / (hi - lo + 1).astype(F32)
                    for lb in range(n_lb):
                        tot, mid = None, None
                        for d in range(-(k // 2), k - k // 2):
                            x = strided(u_scr, lb, HALO + r + c + d)
                            tot = x if tot is None else tot + x
                            if d == 0:
                                mid = x
                        put(mxf_scr, lb, r + c, tot * inv_cnt - mid)

    for lb in range(n_lb):
        lanes = slice(lb * LANES, (lb + 1) * LANES)
        for r in range(0, tr, ROW_CHUNK):
            rows = slice(r, r + ROW_CHUNK)
            cv_ref[rows, lanes] = cvf_scr[lb, rows, :].astype(cv_ref.dtype)
            mx_ref[rows, lanes] = mxf_scr[lb, rows, :].astype(mx_ref.dtype)


def _inproj_gate_kernel(h_ref, w_ref, *refs, seq_len, group_size):
    seq_in, (p_ref, cv_ref, mx_ref), scratch = refs[:11], refs[11:14], refs[14:]
    p_ref[...] = jnp.dot(h_ref[...], w_ref[...], preferred_element_type=F32).astype(p_ref.dtype)
    _seqmix(*seq_in, cv_ref, mx_ref, *scratch, seq_len=seq_len, group_size=group_size)


def _inproj_gate(h, w_in, p_seq, conv_w, conv_b, *, seq_len, width, group_size):
    t, d = h.shape
    tm = _tile(seq_len, 1024)
    tn = _tile(width, 1024)
    nseg = width // tn
    n_steps = 4 * nseg
    tc = width // n_steps
    assert tc % LANES == 0 and group_size % tc == 0 and tm % ROW_CHUNK == 0
    nc = width // tc
    hb = tm // HALO
    last_hb = t // HALO - 1
    w_col = lambda i, j: (0, nseg + j + jnp.where(j >= nseg, 2 * nseg, 0))

    def main(seg):
        return pl.BlockSpec((tm, tc), lambda i, j: (i, seg * nc + j))

    def prev(seg):
        return pl.BlockSpec((HALO, tc), lambda i, j: (jnp.maximum(i * hb - 1, 0), seg * nc + j))

    def nxt(seg):
        return pl.BlockSpec((HALO, tc), lambda i, j: (jnp.minimum((i + 1) * hb, last_hb), seg * nc + j))

    seg_u, seg_a, seg_b = 0, 1, 2
    n_lb = tc // LANES
    kernel = functools.partial(_inproj_gate_kernel, seq_len=seq_len, group_size=group_size)
    return pl.pallas_call(
        kernel,
        out_shape=(jax.ShapeDtypeStruct((t, 4 * width), BF16),
                   jax.ShapeDtypeStruct((t, width), BF16),
                   jax.ShapeDtypeStruct((t, width), BF16)),
        grid=(t // tm, n_steps),
        in_specs=[pl.BlockSpec((tm, d), lambda i, j: (i, 0)),
                  pl.BlockSpec((d, tn), w_col),
                  main(seg_a), prev(seg_a), nxt(seg_a), main(seg_b), prev(seg_b), nxt(seg_b),
                  main(seg_u), prev(seg_u), nxt(seg_u),
                  pl.BlockSpec((conv_w.shape[0], tc), lambda i, j: (0, j)),
                  pl.BlockSpec((1, tc), lambda i, j: (0, j))],
        out_specs=(pl.BlockSpec((tm, tn), lambda i, j: (i, j)),
                   pl.BlockSpec((tm, tc), lambda i, j: (i, j)),
                   pl.BlockSpec((tm, tc), lambda i, j: (i, j))),
        scratch_shapes=[pltpu.VMEM((n_lb, tm + 2 * HALO, LANES), F32),
                        pltpu.VMEM((n_lb, tm + 2 * HALO, LANES), F32),
                        pltpu.VMEM((conv_w.shape[0] + 1, SUBLANES, tc), F32),
                        pltpu.VMEM((n_lb, tm, LANES), F32),
                        pltpu.VMEM((n_lb, tm, LANES), F32)],
        compiler_params=_params(("parallel", "parallel")),
        name="inproj_gate",
    )(h, w_in, p_seq, p_seq, p_seq, p_seq, p_seq, p_seq, p_seq, p_seq, p_seq, conv_w, conv_b)


def _poolmix_kernel(m_ref, w_ref, z_ref, ls_ref, o_ref):
    y = jnp.dot(m_ref[...], w_ref[0], preferred_element_type=F32)
    o_ref[...] = ((y * ls_ref[...]) * _silu(z_ref[...].astype(F32))).astype(o_ref.dtype)


def _poolmix(mixed, w_pool, p_gate, ls_pool):
    t, width = mixed.shape
    n_groups, gs, _ = w_pool.shape
    tm = _tile(t, 1024)
    return pl.pallas_call(
        _poolmix_kernel,
        out_shape=jax.ShapeDtypeStruct((t, width), BF16),
        grid=(t // tm, n_groups),
        in_specs=[
            pl.BlockSpec((tm, gs), lambda i, g: (i, g)),
            pl.BlockSpec((1, gs, gs), lambda i, g: (g, 0, 0)),
            pl.BlockSpec((tm, gs), lambda i, g: (i, g)),
            pl.BlockSpec((1, gs), lambda i, g: (0, g)),
        ],
        out_specs=pl.BlockSpec((tm, gs), lambda i, g: (i, g)),
        compiler_params=_params(("parallel", "parallel")),
        name="poolmix",
    )(mixed, w_pool, p_gate, ls_pool)


def _convnorm_kernel(cv_ref, z_ref, g_ref, b_ref, o_ref):
    width = cv_ref.shape[1]
    n_lb = width // LANES

    def body(r):
        rows = pl.ds(r, 16)

        def block(lb):
            return cv_ref[rows, lb * LANES:(lb + 1) * LANES].astype(F32)

        tot = block(0)
        for lb in range(1, n_lb):
            tot = tot + block(lb)
        mu = jnp.broadcast_to(jnp.sum(tot, axis=-1, keepdims=True) / width, (16, LANES))
        sq = None
        for lb in range(n_lb):
            xc = block(lb) - mu
            sq = xc * xc if sq is None else sq + xc * xc
        var = jnp.sum(sq, axis=-1, keepdims=True) / width
        rstd = jnp.broadcast_to(lax.rsqrt(var + LN_EPS), (16, LANES))
        for lb in range(n_lb):
            lanes = slice(lb * LANES, (lb + 1) * LANES)
            y = ((block(lb) - mu) * rstd) * g_ref[:, lanes] + b_ref[:, lanes]
            z = z_ref[rows, lanes].astype(F32)
            o_ref[rows, lanes] = (_silu(y) * _silu(z)).astype(o_ref.dtype)
    _row_loop(cv_ref.shape[0], 16, body, unroll=4)


def _convnorm(cv, p_gate, ln_g, ln_b):
    t, width = cv.shape
    tr = _tile(t, 256)
    return pl.pallas_call(
        _convnorm_kernel,
        out_shape=jax.ShapeDtypeStruct((t, width), BF16),
        grid=(t // tr,),
        in_specs=[
            pl.BlockSpec((tr, width), lambda i: (i, 0)),
            pl.BlockSpec((tr, width), lambda i: (i, 1)),
            pl.BlockSpec((1, width), lambda i: (0, 0)),
            pl.BlockSpec((1, width), lambda i: (0, 0)),
        ],
        out_specs=pl.BlockSpec((tr, width), lambda i: (i, 0)),
        compiler_params=_params(("parallel",)),
        name="convnorm",
    )(cv, p_gate, ln_g, ln_b)


def _proj_kernel(ap_ref, ac_ref, wpp_ref, wpc_ref, bc_ref, gp_ref, gc_ref, o_ref):
    y_pool = jnp.dot(ap_ref[...], wpp_ref[...], preferred_element_type=F32)
    y_conv = jnp.dot(ac_ref[...], wpc_ref[...], preferred_element_type=F32) + bc_ref[...]
    merged = _sigmoid(gp_ref[...].astype(F32)) * y_pool + _sigmoid(gc_ref[...].astype(F32)) * y_conv
    o_ref[...] = merged.astype(o_ref.dtype)


def _proj(act_pool, act_conv, wpp, wpc, b_proj_conv, p_gate):
    t, width = act_pool.shape
    d = wpp.shape[1]
    tm, tn = _tile(t, 512), _tile(d, 512)
    nd = d // tn
    seg_gp = (2 * width) // tn
    seg_gc = seg_gp + nd
    return pl.pallas_call(
        _proj_kernel,
        out_shape=jax.ShapeDtypeStruct((t, d), BF16),
        grid=(t // tm, nd),
        in_specs=[
            pl.BlockSpec((tm, width), lambda i, j: (i, 0)),
            pl.BlockSpec((tm, width), lambda i, j: (i, 0)),
            pl.BlockSpec((width, tn), lambda i, j: (0, j)),
            pl.BlockSpec((width, tn), lambda i, j: (0, j)),
            pl.BlockSpec((1, tn), lambda i, j: (0, j)),
            pl.BlockSpec((tm, tn), lambda i, j: (i, seg_gp + j)),
            pl.BlockSpec((tm, tn), lambda i, j: (i, seg_gc + j)),
        ],
        out_specs=pl.BlockSpec((tm, tn), lambda i, j: (i, j)),
        compiler_params=_params(("parallel", "arbitrary")),
        name="proj",
    )(act_pool, act_conv, wpp, wpc, b_proj_conv, p_gate, p_gate)


def _out_kernel(m_ref, w_ref, x_ref, g_ref, o_ref, y_scr):
    nd, tm, tn = y_scr.shape
    j = pl.program_id(1)
    y_scr[j] = x_ref[...] + jnp.dot(m_ref[...], w_ref[...], preferred_element_type=F32)

    @pl.when(j == nd - 1)
    def _():
        def body(r):
            rows = pl.ds(r, 16)
            ss = jnp.zeros((16, 1), F32)
            for c in range(nd):
                v = y_scr[c, rows, :]
                ss = ss + jnp.sum(v * v, axis=-1, keepdims=True)
            inv = lax.rsqrt(ss / (nd * tn) + RMS_EPS)
            for c in range(nd):
                cols = slice(c * tn, (c + 1) * tn)
                o_ref[rows, cols] = (y_scr[c, rows, :] * inv) * g_ref[:, cols]
        _row_loop(tm, 16, body, unroll=2)


def _out(merged, w_out, x, g_final):
    t, d = x.shape
    tm, tn = _tile(t, 512), _tile(d, 1024)
    return pl.pallas_call(
        _out_kernel,
        out_shape=jax.ShapeDtypeStruct((t, d), F32),
        grid=(t // tm, d // tn),
        in_specs=[
            pl.BlockSpec((tm, d), lambda i, j: (i, 0)),
            pl.BlockSpec((d, tn), lambda i, j: (0, j)),
            pl.BlockSpec((tm, tn), lambda i, j: (i, j)),
            pl.BlockSpec((1, d), lambda i, j: (0, 0)),
        ],
        out_specs=pl.BlockSpec((tm, d), lambda i, j: (i, 0)),
        scratch_shapes=[pltpu.VMEM((d // tn, tm, tn), F32)],
        compiler_params=_params(("parallel", "arbitrary"), OUT_VMEM_LIMIT_BYTES),
        name="outproj",
    )(merged, w_out, x, g_final)


def _layer(x, weights):
    (g_norm, w_in, w_pool, ls_pool, wpp, conv_w, conv_b, ln_g, ln_b, wpc, b_proj_conv, w_out,
     g_final) = weights
    batch, seq_len, d = x.shape
    width = wpp.shape[0]
    group_size = w_pool.shape[1]
    xt = x.reshape(batch * seq_len, d)
    h = _rmsnorm(xt, g_norm)
    p_seq = _inproj_seq(h, w_in, width)
    p_gate, cv, mixed = _inproj_gate(h, w_in, p_seq, conv_w, conv_b, seq_len=seq_len, width=width,
                                     group_size=group_size)
    act_pool = _poolmix(mixed, w_pool, p_gate, ls_pool)
    act_conv = _convnorm(cv, p_gate, ln_g, ln_b)
    merged = _proj(act_pool, act_conv, wpp, wpc, b_proj_conv, p_gate)
    return _out(merged, w_out, xt, g_final).reshape(batch, seq_len, d)


def kernel(x_prompt, x_sample, g_norm, w_in, w_pool, ls_pool, w_proj_pool, conv_w, conv_b, ln_g, ln_b,
           w_proj_conv, b_proj_conv, w_out, g_final):
    depth = w_in.shape[0]
    d = x_prompt.shape[-1]
    width = w_proj_pool.shape[1]
    assert depth == 1 and w_proj_conv.shape[1] == width == d, "segment layout assumes equal branch widths"
    assert w_in.shape[2] == 5 * width + 2 * d and conv_w.shape[1] // 2 < HALO
    assert w_pool.shape[1] == len(POOL_WINDOWS) and max(POOL_WINDOWS) // 2 <= HALO
    weights = (
        g_norm[0][None, :], w_in[0].astype(BF16), w_pool[0].astype(BF16), ls_pool[0][None, :],
        w_proj_pool[0].astype(BF16), conv_w[0], conv_b[0][None, :], ln_g[0][None, :], ln_b[0][None, :],
        w_proj_conv[0].astype(BF16), b_proj_conv[0][None, :], w_out[0].astype(BF16), g_final[None, :],
    )
    return (_layer(x_prompt, weights), _layer(x_sample, weights))
```
